```python
import jax, jax.numpy as jnp
from jax import lax
import numpy as np

D_MODEL = 1024
BATCH = 32
SEQ = 256
DEPTH = 2
DEC_BATCH = 8
DEC_SEQ = 4096
PAST_LEN = 256

GRID_W = 64
N_EVEN = (DEPTH + 1) // 2
N_ODD = DEPTH // 2
EPS = 1e-6
MLA_HEADS = 4
Q_LORA = 256
KV_LORA = 128
QK_NOPE = 128
QK_ROPE = 64
V_DIM = 128
MLA_WIDTH = MLA_HEADS * V_DIM
ROPE_THETA = 10000.0
Q_BLOCK = 128
POOL_WINDOWS = (2, 4, 8, 16)
POOL_GROUP = 128
POOL_WIDTH = len(POOL_WINDOWS) * POOL_GROUP
MIX_WIDTH_AB = MLA_WIDTH + POOL_WIDTH
SPLITS_AB = (Q_LORA, Q_LORA + KV_LORA, Q_LORA + KV_LORA + QK_ROPE,
             Q_LORA + KV_LORA + QK_ROPE + MLA_WIDTH,
             Q_LORA + KV_LORA + QK_ROPE + MLA_WIDTH + POOL_WIDTH)
IN_AB = Q_LORA + KV_LORA + QK_ROPE + MLA_WIDTH + 2 * POOL_WIDTH
CHUNK = 128
SGU_GROUPS = 4
C_WIDTH = D_MODEL
SGU_GROUP_DIM = C_WIDTH // SGU_GROUPS
IN_C = 3 * C_WIDTH

kernel_name = "hybrid_mla_pool_gmlp_diffusion_step"


def rmsnorm(x, g):
    xf = x.astype(jnp.float32)
    y = xf * lax.rsqrt(jnp.mean(xf * xf, axis=-1, keepdims=True) + EPS)
    return (y * g.astype(jnp.float32)).astype(x.dtype)


def layernorm(x, g, b):
    xf = x.astype(jnp.float32)
    mu = jnp.mean(xf, axis=-1, keepdims=True)
    var = jnp.mean(jnp.square(xf - mu), axis=-1, keepdims=True)
    y = (xf - mu) * lax.rsqrt(var + EPS)
    return (y * g.astype(jnp.float32) + b.astype(jnp.float32)).astype(x.dtype)


def rope_1d(x, pos):
    half = x.shape[-1] // 2
    inv = ROPE_THETA ** (-jnp.arange(half, dtype=jnp.float32) / half)
    ang = pos.astype(jnp.float32)[:, None] * inv
    ang = ang.reshape((ang.shape[0],) + (1,) * (x.ndim - 3) + (half,))
    cos, sin = jnp.cos(ang), jnp.sin(ang)
    xf = x.astype(jnp.float32)
    x1, x2 = xf[..., :half], xf[..., half:]
    return jnp.concatenate([x1 * cos - x2 * sin, x1 * sin + x2 * cos], axis=-1).astype(x.dtype)


def axial_rope(x, rows):
    row = jnp.repeat(jnp.arange(rows), GRID_W)
    col = jnp.tile(jnp.arange(GRID_W), rows)
    a = QK_ROPE // 2
    return jnp.concatenate([rope_1d(x[..., :a], row), rope_1d(x[..., a:], col)], axis=-1)


def mla_attend(q_nope, q_pe, k_nope, k_pe, v):
    B, Tq, H, _ = q_nope.shape
    nb = Tq // Q_BLOCK
    sm_scale = (QK_NOPE + QK_ROPE) ** -0.5

    def block(args):
        qn, qp = args
        s = (jnp.einsum('bqhd,bkhd->bhqk', qn, k_nope, preferred_element_type=jnp.float32)
             + jnp.einsum('bqhr,bkr->bhqk', qp, k_pe, preferred_element_type=jnp.float32))
        p = jax.nn.softmax(s * sm_scale, axis=-1).astype(v.dtype)
        return jnp.einsum('bhqk,bkhd->bqhd', p, v)

    qn_b = q_nope.reshape(B, nb, Q_BLOCK, H, QK_NOPE).transpose(1, 0, 2, 3, 4)
    qp_b = q_pe.reshape(B, nb, Q_BLOCK, H, QK_ROPE).transpose(1, 0, 2, 3, 4)
    o = lax.map(block, (qn_b, qp_b))
    return o.transpose(1, 0, 2, 3, 4).reshape(B, Tq, H * V_DIM)


def multiscale_pool(u, w_pool, pool_scale):
    B, T, _ = u.shape
    uf = u.astype(jnp.float32)
    cs = jnp.concatenate([jnp.zeros((B, 1, POOL_WIDTH), jnp.float32), jnp.cumsum(uf, axis=1)], axis=1)
    t = jnp.arange(T)
    outs = []
    for g, w in enumerate(POOL_WINDOWS):
        lo = jnp.clip(t - w // 2, 0, T)
        hi = jnp.clip(t + w - w // 2, 0, T)
        sl = slice(g * POOL_GROUP, (g + 1) * POOL_GROUP)
        csg = cs[:, :, sl]
        mean = (csg[:, hi] - csg[:, lo]) / (hi - lo).astype(jnp.float32)[None, :, None]
        d = (mean - uf[:, :, sl]).astype(u.dtype)
        outs.append(d @ w_pool[g])
    return jnp.concatenate(outs, axis=-1) * pool_scale


def mixer_ap(h, p, ctx_ckv=None, ctx_kpe=None):
    w_in, q_norm, w_uq, kv_norm, w_ukv, w_pool, pool_scale, w_out = p
    B, T, _ = h.shape
    q_lat, kv_lat, k_pe, gate_a, pool_in, gate_b = jnp.split(h @ w_in, SPLITS_AB, axis=-1)
    q = (rmsnorm(q_lat, q_norm) @ w_uq).reshape(B, T, MLA_HEADS, QK_NOPE + QK_ROPE)
    q_nope, q_pe = q[..., :QK_NOPE], q[..., QK_NOPE:]
    ckv = rmsnorm(kv_lat, kv_norm)
    if ctx_ckv is None:
        keys_ckv, keys_pe = ckv, k_pe
    else:
        rows = T // GRID_W
        q_pe = axial_rope(q_pe, rows)
        keys_ckv = jnp.concatenate([ckv, ctx_ckv.astype(ckv.dtype)], axis=1)
        keys_pe = jnp.concatenate([axial_rope(k_pe, rows), ctx_kpe.astype(k_pe.dtype)], axis=1)
    Tk = keys_ckv.shape[1]
    kv = (keys_ckv @ w_ukv).reshape(B, Tk, MLA_HEADS, QK_NOPE + V_DIM)
    k_nope, v = kv[..., :QK_NOPE], kv[..., QK_NOPE:]
    o_a = mla_attend(q_nope, q_pe, k_nope, keys_pe, v)
    o_b = multiscale_pool(pool_in, w_pool, pool_scale)
    mixed = jnp.concatenate([o_a * jax.nn.silu(gate_a), o_b * jax.nn.silu(gate_b)], axis=-1)
    return mixed @ w_out, ckv, k_pe


def mixer_c(h, p):
    w_in, ln_g, ln_b, w_s, b_s, w_out = p
    B, T, _ = h.shape
    u, v, gate = jnp.split(h @ w_in, 3, axis=-1)
    u = jax.nn.gelu(u)
    v = layernorm(jax.nn.gelu(v), ln_g, ln_b)
    vb = v.reshape(B, T // CHUNK, CHUNK, SGU_GROUPS, SGU_GROUP_DIM)
    sv = jnp.einsum('gpq,bnqgc->bnpgc', w_s, vb) + b_s.T[:, :, None]
    sv = sv.reshape(B, T, C_WIDTH)
    return (u * sv * jax.nn.silu(gate)) @ w_out


def trunk(x, cond, layer_p, even_p, odd_p, cache_ckv, cache_kpe):
    w_ada, b_ada, norm_pre, norm_post = layer_p
    ckvs, kpes = [], []
    for l in range(DEPTH):
        shift, scale, gate = jnp.split(jax.nn.silu(cond) @ w_ada[l] + b_ada[l], 3, axis=-1)
        h = rmsnorm(x, norm_pre[l]) * (1 + scale) + shift
        if l % 2 == 0:
            i = l // 2
            p = tuple(a[i] for a in even_p)
            if cache_ckv is None:
                out, ckv, kpe = mixer_ap(h, p)
                ckvs.append(ckv)
                kpes.append(kpe)
            else:
                out, _, _ = mixer_ap(h, p, cache_ckv[:, i], cache_kpe[:, i])
        else:
            out = mixer_c(h, tuple(a[l // 2] for a in odd_p))
        x = x + gate * rmsnorm(out, norm_post[l])
    return x, ckvs, kpes


def setup_inputs(seed: int = 0) -> dict:
    key = jax.random.key(seed)
    ks = jax.random.split(key, 32)
    f32 = jnp.float32

    def nrm(k, shape, s=1.0):
        return jax.random.normal(k, shape, f32) * s

    def gain(k, shape):
        return 1.0 + 0.02 * jax.random.normal(k, shape, f32)

    D = D_MODEL
    return {
        "x_prompt": nrm(ks[0], (BATCH, SEQ, D)),
        "x_sample": nrm(ks[1], (DEC_BATCH, DEC_SEQ, D)),
        "cache_ckv": nrm(ks[2], (DEC_BATCH, N_EVEN, PAST_LEN, KV_LORA)),
        "cache_kpe": nrm(ks[3], (DEC_BATCH, N_EVEN, PAST_LEN, QK_ROPE)),
        "c": nrm(ks[4], (DEC_BATCH, D)),
        "c_ctx": nrm(ks[5], (D,)),
        "w_ada": nrm(ks[6], (DEPTH, D, 3 * D), 0.5 * D ** -0.5),
        "b_ada": nrm(ks[7], (DEPTH, 3 * D), 0.01),
        "norm_pre": gain(ks[8], (DEPTH, D)),
        "norm_post": gain(ks[9], (DEPTH, D)),
        "w_in_ap": nrm(ks[10], (N_EVEN, D, IN_AB), D ** -0.5),
        "q_norm": gain(ks[11], (N_EVEN, Q_LORA)),
        "w_uq": nrm(ks[12], (N_EVEN, Q_LORA, MLA_HEADS * (QK_NOPE + QK_ROPE)), Q_LORA ** -0.5),
        "kv_norm": gain(ks[13], (N_EVEN, KV_LORA)),
        "w_ukv": nrm(ks[14], (N_EVEN, KV_LORA, MLA_HEADS * (QK_NOPE + V_DIM)), KV_LORA ** -0.5),
        "w_pool": nrm(ks[15], (N_EVEN, len(POOL_WINDOWS), POOL_GROUP, POOL_GROUP), POOL_GROUP ** -0.5),
        "pool_scale": gain(ks[16], (N_EVEN, POOL_WIDTH)),
        "w_out_ap": nrm(ks[17], (N_EVEN, MIX_WIDTH_AB, D), MIX_WIDTH_AB ** -0.5),
        "w_in_c": nrm(ks[18], (N_ODD, D, IN_C), D ** -0.5),
        "sgu_ln_g": gain(ks[19], (N_ODD, C_WIDTH)),
        "sgu_ln_b": nrm(ks[20], (N_ODD, C_WIDTH), 0.01),
        "w_s": nrm(ks[21], (N_ODD, SGU_GROUPS, CHUNK, CHUNK), CHUNK ** -0.5),
        "b_s": gain(ks[22], (N_ODD, SGU_GROUPS, CHUNK)),
        "w_out_c": nrm(ks[23], (N_ODD, C_WIDTH, D), C_WIDTH ** -0.5),
    }


def reference(x_prompt, x_sample, cache_ckv, cache_kpe, c, c_ctx, w_ada, b_ada, norm_pre, norm_post,
              w_in_ap, q_norm, w_uq, kv_norm, w_ukv, w_pool, pool_scale, w_out_ap,
              w_in_c, sgu_ln_g, sgu_ln_b, w_s, b_s, w_out_c):
    layer_p = (w_ada, b_ada, norm_pre, norm_post)
    even_p = (w_in_ap, q_norm, w_uq, kv_norm, w_ukv, w_pool, pool_scale, w_out_ap)
    odd_p = (w_in_c, sgu_ln_g, sgu_ln_b, w_s, b_s, w_out_c)
    y_prompt, ckvs, kpes = trunk(x_prompt, c_ctx[None, None, :], layer_p, even_p, odd_p, None, None)
    new_ckv = jnp.stack(ckvs, axis=1)
    new_kpe = jnp.stack(kpes, axis=1)
    y_sample, _, _ = trunk(x_sample, c[:, None, :], layer_p, even_p, odd_p, cache_ckv, cache_kpe)
    return (y_prompt, y_sample, new_ckv, new_kpe)
```

```python
import functools

import jax
import jax.numpy as jnp
from jax import lax
from jax.experimental import pallas as pl
from jax.experimental.pallas import tpu as pltpu

D_MODEL = 1024
EPS = 1e-6
MLA_HEADS = 4
Q_LORA = 256
KV_LORA = 128
QK_NOPE = 128
QK_ROPE = 64
V_DIM = 128
ROPE_THETA = 10000.0
GRID_W = 64
POOL_WINDOWS = (2, 4, 8, 16)
POOL_GROUP = 128
POOL_WIDTH = len(POOL_WINDOWS) * POOL_GROUP
MLA_WIDTH = MLA_HEADS * V_DIM
CHUNK = 128
SGU_GROUPS = 4
SGU_GROUP_DIM = D_MODEL // SGU_GROUPS
SM_SCALE = (QK_NOPE + QK_ROPE) ** -0.5

QK_PAD = 256
POOL_HALO = 8
MOD_ROWS = 16
VMEM_LIMIT = 56 * 1024 * 1024

_C_QLAT = 0
_C_KVLAT = _C_QLAT + Q_LORA
_C_GATE_A = _C_KVLAT + KV_LORA
_C_POOL = _C_GATE_A + MLA_WIDTH
_C_GATE_B = _C_POOL + POOL_WIDTH
_C_KPE = _C_GATE_B + POOL_WIDTH
_C_END = _C_KPE + 2 * QK_ROPE

BF16 = jnp.bfloat16
F32 = jnp.float32


def _dot(a, b):
    return jnp.dot(a, b, preferred_element_type=F32)


def _dot_nt(a, b):
    return lax.dot_general(a, b, (((1,), (1,)), ((), ())), preferred_element_type=F32)


def _rms(x, g):
    return x * lax.rsqrt(jnp.mean(x * x, axis=-1, keepdims=True) + EPS) * g


def _ada_kernel(c_ref, w_ref, b_ref, o_ref):
    a = jax.nn.silu(c_ref[...]).astype(BF16)
    o_ref[0] = _dot(a, w_ref[0].astype(BF16)) + b_ref[0]


def _ada(cond, w_ada, b_ada):
    depth, d, n = w_ada.shape
    bn = 512
    return pl.pallas_call(
        _ada_kernel,
        grid=(depth, n // bn),
        in_specs=[
            pl.BlockSpec((MOD_ROWS, d), lambda l, j: (0, 0)),
            pl.BlockSpec((1, d, bn), lambda l, j: (l, 0, j)),
            pl.BlockSpec((1, 1, bn), lambda l, j: (l, 0, j)),
        ],
        out_specs=pl.BlockSpec((1, MOD_ROWS, bn), lambda l, j: (l, 0, j)),
        out_shape=jax.ShapeDtypeStruct((depth, MOD_ROWS, n), F32),
        name="ada_mod",
    )(cond, w_ada, b_ada.reshape(depth, 1, n))


def _front_kernel(*refs, tm, seq, halo, rope, emit_cache):
    it = iter(refs)
    x_ref = next(it)
    xp_ref = next(it) if halo else None
    xn_ref = next(it) if halo else None
    mod_ref = next(it)
    npre_ref = next(it)
    win_ref = next(it)
    qn_ref = next(it)
    wuq_ref = next(it)
    kvn_ref = next(it)
    wuk_ref = next(it)
    wuvt_ref = next(it)
    wpool_ref = next(it)
    pscale_ref = next(it)
    cos_ref = next(it) if rope else None
    sin_ref = next(it) if rope else None
    q_ref = next(it)
    k_ref = next(it)
    vt_ref = next(it)
    ga_ref = next(it)
    pp_ref = next(it)
    ckv_ref = next(it) if emit_cache else None
    kpe_ref = next(it) if emit_cache else None

    i = pl.program_id(0)
    nt = pl.num_programs(0)
    d = D_MODEL
    shift = mod_ref[0, :, 0:d]
    scale = mod_ref[0, :, d:2 * d]
    npre = npre_ref[...]

    def modulate(xv):
        return (_rms(xv, npre) * (1.0 + scale) + shift).astype(BF16)

    def rotate(blk):
        if not rope:
            return blk
        return blk * cos_ref[...] + pltpu.roll(blk, QK_ROPE, axis=1) * sin_ref[...]

    h = modulate(x_ref[0])

    q_lat = _dot(h, win_ref[:, _C_QLAT:_C_KVLAT])
    qn = _rms(q_lat, qn_ref[...]).astype(BF16)
    for hd in range(MLA_HEADS):
        base = hd * QK_PAD
        q_ref[0, :, base:base + QK_NOPE] = (
            _dot(qn, wuq_ref[:, base:base + QK_NOPE]) * SM_SCALE).astype(BF16)
        qpe = _dot(qn, wuq_ref[:, base + QK_NOPE:base + QK_PAD]) * SM_SCALE
        q_ref[0, :, base + QK_NOPE:base + QK_PAD] = rotate(qpe).astype(BF16)

    ckv = _rms(_dot(h, win_ref[:, _C_KVLAT:_C_GATE_A]), kvn_ref[...])
    kpe2 = _dot(h, win_ref[:, _C_KPE:_C_END])
    if emit_cache:
        ckv_ref[0] = ckv
        kpe_ref[0] = kpe2[:, 0:QK_ROPE]
    ckv_b = ckv.astype(BF16)
    kpe_b = rotate(kpe2).astype(BF16)
    vt = _dot_nt(wuvt_ref[...], ckv_b)
    for hd in range(MLA_HEADS):
        k_ref[0, hd, :, 0:QK_NOPE] = _dot(
            ckv_b, wuk_ref[:, hd * QK_NOPE:(hd + 1) * QK_NOPE]).astype(BF16)
        k_ref[0, hd, :, QK_NOPE:QK_PAD] = kpe_b
        vt_ref[0, hd, 0] = vt[hd * V_DIM:(hd + 1) * V_DIM].astype(BF16)

    ga_ref[0] = jax.nn.silu(_dot(h, win_ref[:, _C_GATE_A:_C_POOL])).astype(BF16)

    u = _dot(h, win_ref[:, _C_POOL:_C_GATE_B])
    if halo:
        hh = modulate(jnp.concatenate([xp_ref[0], xn_ref[0]], axis=0))
        uh = _dot(hh, win_ref[:, _C_POOL:_C_GATE_B])
        up = jnp.where(i > 0, uh[0:POOL_HALO], 0.0)
        un = jnp.where(i < nt - 1, uh[POOL_HALO:2 * POOL_HALO], 0.0)
    else:
        up = jnp.zeros((POOL_HALO, POOL_WIDTH), F32)
        un = up
    ue = jnp.concatenate([up, u, un], axis=0)
    ext = tm + 2 * POOL_HALO
    t = i * tm + lax.broadcasted_iota(jnp.int32, (tm, 1), 0)
    gate_b = _dot(h, win_ref[:, _C_GATE_B:_C_KPE])
    for g, w in enumerate(POOL_WINDOWS):
        sl = slice(g * POOL_GROUP, (g + 1) * POOL_GROUP)
        p = ue[:, sl]
        k = 1
        while k < w:
            p = p + pltpu.roll(p, k, axis=0)
            k *= 2
        lead = w // 2 - 1
        if lead:
            p = pltpu.roll(p, ext - lead, axis=0)
        wsum = p[POOL_HALO:POOL_HALO + tm]
        cnt = (jnp.minimum(t + w // 2, seq) - jnp.maximum(t - w // 2, 0)).astype(F32)
        dlt = (wsum * (1.0 / cnt) - u[:, sl]).astype(BF16)
        og = _dot(dlt, wpool_ref[g]) * pscale_ref[:, sl]
        pp_ref[0, :, sl] = (og * jax.nn.silu(gate_b[:, sl])).astype(BF16)


def _front(x, mod, mod_row, npre, w_in, qn, w_uq, kvn, w_uk, w_uvt, w_pool, pscale, tables, *, tm, emit_cache):
    b, seq, d = x.shape
    nt = seq // tm
    halo = nt > 1
    rope = tables is not None
    hb = tm // POOL_HALO
    last = seq // POOL_HALO - 1

    def const(shape):
        return pl.BlockSpec(shape, lambda i, j: (0,) * len(shape))

    in_specs = [pl.BlockSpec((1, tm, d), lambda i, j: (j, i, 0))]
    args = [x]
    if halo:
        in_specs += [
            pl.BlockSpec((1, POOL_HALO, d), lambda i, j: (j, jnp.maximum(i * hb - 1, 0), 0)),
            pl.BlockSpec((1, POOL_HALO, d), lambda i, j: (j, jnp.minimum((i + 1) * hb, last), 0)),
        ]
        args += [x, x]
    in_specs += [
        pl.BlockSpec((1, 1, 3 * d), lambda i, j: (mod_row(j), 0, 0)),
        const((1, d)), const(w_in.shape), const((1, Q_LORA)), const(w_uq.shape),
        const((1, KV_LORA)), const(w_uk.shape), const(w_uvt.shape), const(w_pool.shape),
        const((1, POOL_WIDTH)),
    ]
    args += [mod, npre, w_in, qn, w_uq, kvn, w_uk, w_uvt, w_pool, pscale]
    if rope:
        in_specs += [pl.BlockSpec((tm, 2 * QK_ROPE), lambda i, j: (i, 0))] * 2
        args += list(tables)

    out_specs = [
        pl.BlockSpec((1, tm, MLA_HEADS * QK_PAD), lambda i, j: (j, i, 0)),
        pl.BlockSpec((1, MLA_HEADS, tm, QK_PAD), lambda i, j: (j, 0, i, 0)),
        pl.BlockSpec((1, MLA_HEADS, 1, V_DIM, tm), lambda i, j: (j, 0, i, 0, 0)),
        pl.BlockSpec((1, tm, MLA_WIDTH), lambda i, j: (j, i, 0)),
        pl.BlockSpec((1, tm, POOL_WIDTH), lambda i, j: (j, i, 0)),
    ]
    out_shape = [
        jax.ShapeDtypeStruct((b, seq, MLA_HEADS * QK_PAD), BF16),
        jax.ShapeDtypeStruct((b, MLA_HEADS, seq, QK_PAD), BF16),
        jax.ShapeDtypeStruct((b, MLA_HEADS, nt, V_DIM, tm), BF16),
        jax.ShapeDtypeStruct((b, seq, MLA_WIDTH), BF16),
        jax.ShapeDtypeStruct((b, seq, POOL_WIDTH), BF16),
    ]
    if emit_cache:
        out_specs += [
            pl.BlockSpec((1, tm, KV_LORA), lambda i, j: (j, i, 0)),
            pl.BlockSpec((1, tm, QK_ROPE), lambda i, j: (j, i, 0)),
        ]
        out_shape += [
            jax.ShapeDtypeStruct((b, seq, KV_LORA), F32),
            jax.ShapeDtypeStruct((b, seq, QK_ROPE), F32),
        ]
    return pl.pallas_call(
        functools.partial(_front_kernel, tm=tm, seq=seq, halo=halo, rope=rope, emit_cache=emit_cache),
        grid=(nt, b),
        in_specs=in_specs,
        out_specs=out_specs,
        out_shape=out_shape,
        compiler_params=pltpu.CompilerParams(
            dimension_semantics=("arbitrary", "arbitrary"), vmem_limit_bytes=VMEM_LIMIT),
        name="mla_pool_front",
    )(*args)


def _ctx_kv_kernel(ckv_ref, kpe_ref, wuk_ref, wuvt_ref, k_ref, vt_ref):
    ckv_b = ckv_ref[0].astype(BF16)
    kpe_b = kpe_ref[0].astype(BF16)
    vt = _dot_nt(wuvt_ref[...], ckv_b)
    for hd in range(MLA_HEADS):
        k_ref[0, hd, :, 0:QK_NOPE] = _dot(
            ckv_b, wuk_ref[:, hd * QK_NOPE:(hd + 1) * QK_NOPE]).astype(BF16)
        k_ref[0, hd, :, QK_NOPE:QK_PAD] = kpe_b
        vt_ref[0, hd] = vt[hd * V_DIM:(hd + 1) * V_DIM].astype(BF16)


def _ctx_kv(ckv, kpe_pad, w_uk, w_uvt):
    b, past, _ = ckv.shape
    return pl.pallas_call(
        _ctx_kv_kernel,
        grid=(b,),
        in_specs=[
            pl.BlockSpec((1, past, KV_LORA), lambda j: (j, 0, 0)),
            pl.BlockSpec((1, past, 2 * QK_ROPE), lambda j: (j, 0, 0)),
            pl.BlockSpec(w_uk.shape, lambda j: (0, 0)),
            pl.BlockSpec(w_uvt.shape, lambda j: (0, 0)),
        ],
        out_specs=[
            pl.BlockSpec((1, MLA_HEADS, past, QK_PAD), lambda j: (j, 0, 0, 0)),
            pl.BlockSpec((1, MLA_HEADS, V_DIM, past), lambda j: (j, 0, 0, 0)),
        ],
        out_shape=[
            jax.ShapeDtypeStruct((b, MLA_HEADS, past, QK_PAD), BF16),
            jax.ShapeDtypeStruct((b, MLA_HEADS, V_DIM, past), BF16),
        ],
        name="ctx_kv",
    )(ckv, kpe_pad, w_uk, w_uvt)


def _attn_kernel(*refs, tq, kc, nchunks, has_ctx):
    it = iter(refs)
    q_ref = next(it)
    k_ref = next(it)
    vt_ref = next(it)
    kc_ref = next(it) if has_ctx else None
    vtc_ref = next(it) if has_ctx else None
    ga_ref = next(it)
    pp_ref = next(it)
    x_ref = next(it)
    mod_ref = next(it)
    npost_ref = next(it)
    wout_ref = next(it)
    o_ref = next(it)
    s_scr = next(it)
    mix_scr = next(it)

    d = D_MODEL
    sub = 8
    neg = jnp.full((sub, tq), -jnp.inf, F32)

    for hd in range(MLA_HEADS):
        qh = q_ref[0, :, hd * QK_PAD:(hd + 1) * QK_PAD]

        def score_chunk(c, m8, hd=hd, qh=qh):
            r0 = pl.multiple_of(c * kc, kc)
            s = _dot_nt(k_ref[0, hd, pl.ds(r0, kc), :], qh)
            s_scr[pl.ds(r0, kc), :] = s
            return jnp.maximum(m8, jnp.max(s.reshape(kc // sub, sub, tq), axis=0))

        m8 = lax.fori_loop(0, nchunks, score_chunk, neg)
        m = jnp.max(m8, axis=0, keepdims=True)
        if has_ctx:
            sc = _dot_nt(kc_ref[0, hd], qh)
            m = jnp.maximum(m, jnp.max(sc, axis=0, keepdims=True))

        def value_chunk(c, carry, hd=hd, m=m):
            l8, acc = carry
            r0 = pl.multiple_of(c * kc, kc)
            p = jnp.exp(s_scr[pl.ds(r0, kc), :] - m)
            l8 = l8 + jnp.sum(p.reshape(kc // sub, sub, tq), axis=0)
            acc = acc + _dot(vt_ref[0, hd, c], p.astype(BF16))
            return l8, acc

        l8, acc = lax.fori_loop(
            0, nchunks, value_chunk, (jnp.zeros((sub, tq), F32), jnp.zeros((V_DIM, tq), F32)))
        l = jnp.sum(l8, axis=0, keepdims=True)
        if has_ctx:
            pc = jnp.exp(sc - m)
            l = l + jnp.sum(pc, axis=0, keepdims=True)
            acc = acc + _dot(vtc_ref[0, hd], pc.astype(BF16))
        o = (acc * (1.0 / l)).T
        sl = slice(hd * V_DIM, (hd + 1) * V_DIM)
        mix_scr[:, sl] = (o * ga_ref[0, :, sl].astype(F32)).astype(BF16)
    mix_scr[:, MLA_WIDTH:] = pp_ref[0]

    out = _dot(mix_scr[...], wout_ref[...])
    gate = mod_ref[0, :, 2 * d:3 * d]
    o_ref[0] = x_ref[0] + gate * _rms(out, npost_ref[...])


def _attend(q, k, vt, ctx, ga, pp, x, mod, mod_row, npost, w_out, *, tq):
    b, seq, d = x.shape
    nchunks, kc = vt.shape[2], vt.shape[4]
    has_ctx = ctx is not None

    def const(shape):
        return pl.BlockSpec(shape, lambda j, i: (0,) * len(shape))

    in_specs = [
        pl.BlockSpec((1, tq, MLA_HEADS * QK_PAD), lambda j, i: (j, i, 0)),
        pl.BlockSpec((1, MLA_HEADS, seq, QK_PAD), lambda j, i: (j, 0, 0, 0)),
        pl.BlockSpec((1, MLA_HEADS, nchunks, V_DIM, kc), lambda j, i: (j, 0, 0, 0, 0)),
    ]
    args = [q, k, vt]
    if has_ctx:
        kctx, vtctx = ctx
        past = kctx.shape[2]
        in_specs += [
            pl.BlockSpec((1, MLA_HEADS, past, QK_PAD), lambda j, i: (j, 0, 0, 0)),
            pl.BlockSpec((1, MLA_HEADS, V_DIM, past), lambda j, i: (j, 0, 0, 0)),
        ]
        args += [kctx, vtctx]
    in_specs += [
        pl.BlockSpec((1, tq, MLA_WIDTH), lambda j, i: (j, i, 0)),
        pl.BlockSpec((1, tq, POOL_WIDTH), lambda j, i: (j, i, 0)),
        pl.BlockSpec((1, tq, d), lambda j, i: (j, i, 0)),
        pl.BlockSpec((1, 1, 3 * d), lambda j, i: (mod_row(j), 0, 0)),
        const((1, d)), const(w_out.shape),
    ]
    args += [ga, pp, x, mod, npost, w_out]
    return pl.pallas_call(
        functools.partial(_attn_kernel, tq=tq, kc=kc, nchunks=nchunks, has_ctx=has_ctx),
        grid=(b, seq // tq),
        in_specs=in_specs,
        out_specs=pl.BlockSpec((1, tq, d), lambda j, i: (j, i, 0)),
        out_shape=jax.ShapeDtypeStruct((b, seq, d), F32),
        scratch_shapes=[
            pltpu.VMEM((seq, tq), F32),
            pltpu.VMEM((tq, MLA_WIDTH + POOL_WIDTH), BF16),
        ],
        compiler_params=pltpu.CompilerParams(
            dimension_semantics=("arbitrary", "arbitrary"), vmem_limit_bytes=VMEM_LIMIT),
        name="mla_attend_out",
    )(*args)


def _gmlp_kernel(x_ref, mod_ref, npre_ref, win_ref, lng_ref, lnb_ref, ws_ref, bs_ref, wout_ref,
                 npost_ref, o_ref, z_scr, *, tm):
    d = D_MODEL
    x = x_ref[0]
    shift = mod_ref[0, :, 0:d]
    scale = mod_ref[0, :, d:2 * d]
    gate = mod_ref[0, :, 2 * d:3 * d]
    h = (_rms(x, npre_ref[...]) * (1.0 + scale) + shift).astype(BF16)
    u = jax.nn.gelu(_dot(h, win_ref[:, 0:d]))
    v = jax.nn.gelu(_dot(h, win_ref[:, d:2 * d]))
    sg = jax.nn.silu(_dot(h, win_ref[:, 2 * d:3 * d]))
    mu = jnp.mean(v, axis=-1, keepdims=True)
    vc = v - mu
    var = jnp.mean(vc * vc, axis=-1, keepdims=True)
    vn = (vc * lax.rsqrt(var + EPS) * lng_ref[...] + lnb_ref[...]).astype(BF16)
    us = u * sg
    for n in range(tm // CHUNK):
        rows = slice(n * CHUNK, (n + 1) * CHUNK)
        for g in range(SGU_GROUPS):
            cols = slice(g * SGU_GROUP_DIM, (g + 1) * SGU_GROUP_DIM)
            sv = _dot(ws_ref[g], vn[rows, cols]) + bs_ref[:, cols]
            z_scr[rows, cols] = (us[rows, cols] * sv).astype(BF16)
    out = _dot(z_scr[...], wout_ref[...])
    o_ref[0] = x + gate * _rms(out, npost_ref[...])


def _gmlp(x, mod, mod_row, npre, w_in, lng, lnb, w_s, bias, w_out, npost, *, tm):
    b, seq, d = x.shape

    def const(shape):
        return pl.BlockSpec(shape, lambda j, i: (0,) * len(shape))

    return pl.pallas_call(
        functools.partial(_gmlp_kernel, tm=tm),
        grid=(b, seq // tm),
        in_specs=[
            pl.BlockSpec((1, tm, d), lambda j, i: (j, i, 0)),
            pl.BlockSpec((1, 1, 3 * d), lambda j, i: (mod_row(j), 0, 0)),
            const((1, d)), const(w_in.shape), const((1, d)), const((1, d)),
            const(w_s.shape), const(bias.shape), const(w_out.shape), const((1, d)),
        ],
        out_specs=pl.BlockSpec((1, tm, d), lambda j, i: (j, i, 0)),
        out_shape=jax.ShapeDtypeStruct((b, seq, d), F32),
        scratch_shapes=[pltpu.VMEM((tm, d), BF16)],
        compiler_params=pltpu.CompilerParams(
            dimension_semantics=("arbitrary", "arbitrary"), vmem_limit_bytes=VMEM_LIMIT),
        name="gmlp_layer",
    )(x, mod, npre, w_in, lng, lnb, w_s, bias, w_out, npost)


def _swap16(w):
    half = QK_ROPE // 4
    parts = [w[..., k * half:(k + 1) * half] for k in range(4)]
    return jnp.concatenate([parts[1], parts[0], parts[3], parts[2]], axis=-1)


def _rope_tables(seq):
    t = jnp.arange(seq)
    half = QK_ROPE // 4
    inv = ROPE_THETA ** (-jnp.arange(half, dtype=F32) / half)
    ang_r = (t // GRID_W).astype(F32)[:, None] * inv
    ang_c = (t % GRID_W).astype(F32)[:, None] * inv
    cr, sr, cc, sc = jnp.cos(ang_r), jnp.sin(ang_r), jnp.cos(ang_c), jnp.sin(ang_c)
    zero = jnp.zeros((seq, QK_ROPE), F32)
    cos = jnp.concatenate([cr, cr, cc, cc, zero], axis=-1)
    sin = jnp.concatenate([-sr, sr, -sc, sc, zero], axis=-1)
    return cos, sin


def kernel(x_prompt, x_sample, cache_ckv, cache_kpe, c, c_ctx, w_ada, b_ada, norm_pre, norm_post,
           w_in_ap, q_norm, w_uq, kv_norm, w_ukv, w_pool, pool_scale, w_out_ap,
           w_in_c, sgu_ln_g, sgu_ln_b, w_s, b_s, w_out_c):
    d = D_MODEL
    dec_b = x_sample.shape[0]
    ctx_row = dec_b

    cond = jnp.zeros((MOD_ROWS, d), F32).at[:dec_b].set(c).at[ctx_row].set(c_ctx)
    mod = _ada(cond, w_ada, b_ada)
    mod0 = mod[0].reshape(MOD_ROWS, 1, 3 * d)
    mod1 = mod[1].reshape(MOD_ROWS, 1, 3 * d)
    row_sample = lambda j: j
    row_prompt = lambda j: ctx_row

    wi = w_in_ap[0]
    q_lat, kv_lat, k_pe, gate_a, pool_in, gate_b = jnp.split(
        wi, (Q_LORA, Q_LORA + KV_LORA, Q_LORA + KV_LORA + QK_ROPE,
             Q_LORA + KV_LORA + QK_ROPE + MLA_WIDTH,
             Q_LORA + KV_LORA + QK_ROPE + MLA_WIDTH + POOL_WIDTH), axis=1)
    body = [q_lat, kv_lat, gate_a, pool_in, gate_b, k_pe]
    w_in_rope = jnp.concatenate(body + [_swap16(k_pe)], axis=1).astype(BF16)
    w_in_plain = jnp.concatenate(body + [jnp.zeros_like(k_pe)], axis=1).astype(BF16)

    wq = w_uq[0].reshape(Q_LORA, MLA_HEADS, QK_NOPE + QK_ROPE)
    wq_n, wq_p = wq[..., :QK_NOPE], wq[..., QK_NOPE:]
    w_uq_rope = jnp.concatenate([wq_n, wq_p, _swap16(wq_p)], axis=-1).reshape(Q_LORA, -1).astype(BF16)
    w_uq_plain = jnp.concatenate([wq_n, wq_p, jnp.zeros_like(wq_p)], axis=-1).reshape(Q_LORA, -1).astype(BF16)

    wkv = w_ukv[0].reshape(KV_LORA, MLA_HEADS, QK_NOPE + V_DIM)
    w_uk = wkv[..., :QK_NOPE].reshape(KV_LORA, MLA_HEADS * QK_NOPE).astype(BF16)
    w_uvt = wkv[..., QK_NOPE:].transpose(1, 2, 0).reshape(MLA_HEADS * V_DIM, KV_LORA).astype(BF16)

    w_pool_b = w_pool[0].astype(BF16)
    w_out_b = w_out_ap[0].astype(BF16)
    npre0, npost0 = norm_pre[0][None], norm_post[0][None]
    npre1, npost1 = norm_pre[1][None], norm_post[1][None]
    qn, kvn, pscale = q_norm[0][None], kv_norm[0][None], pool_scale[0][None]

    w_in_c_b = w_in_c[0].astype(BF16)
    w_s_b = w_s[0].astype(BF16)
    bias = jnp.repeat(b_s[0].T, SGU_GROUP_DIM, axis=1)
    w_out_c_b = w_out_c[0].astype(BF16)
    lng, lnb = sgu_ln_g[0][None], sgu_ln_b[0][None]

    seq_p = x_prompt.shape[1]
    qp, kp, vtp, gap, ppp, ckv_new, kpe_new = _front(
        x_prompt, mod0, row_prompt, npre0, w_in_plain, qn, w_uq_plain, kvn, w_uk, w_uvt,
        w_pool_b, pscale, None, tm=seq_p, emit_cache=True)
    xp1 = _attend(qp, kp, vtp, None, gap, ppp, x_prompt, mod0, row_prompt, npost0, w_out_b, tq=seq_p)
    y_prompt = _gmlp(xp1, mod1, row_prompt, npre1, w_in_c_b, lng, lnb, w_s_b, bias, w_out_c_b, npost1,
                     tm=seq_p)

    seq_s = x_sample.shape[1]
    kpe_pad = jnp.pad(cache_kpe[:, 0], ((0, 0), (0, 0), (0, QK_ROPE)))
    ctx = _ctx_kv(cache_ckv[:, 0], kpe_pad, w_uk, w_uvt)
    qs, ks, vts, gas, pps = _front(
        x_sample, mod0, row_sample, npre0, w_in_rope, qn, w_uq_rope, kvn, w_uk, w_uvt,
        w_pool_b, pscale, _rope_tables(seq_s), tm=512, emit_cache=False)
    xs1 = _attend(qs, ks, vts, ctx, gas, pps, x_sample, mod0, row_sample, npost0, w_out_b, tq=256)
    y_sample = _gmlp(xs1, mod1, row_sample, npre1, w_in_c_b, lng, lnb, w_s_b, bias, w_out_c_b, npost1,
                     tm=512)

    return (y_prompt, y_sample, ckv_new[:, None], kpe_new[:, None])
```

```python
import functools

import jax
import jax.numpy as jnp
from jax import lax
from jax.experimental import pallas as pl
from jax.experimental.pallas import tpu as pltpu

D_MODEL = 1024
EPS = 1e-6
MLA_HEADS = 4
Q_LORA = 256
KV_LORA = 128
QK_NOPE = 128
QK_ROPE = 64
V_DIM = 128
ROPE_THETA = 10000.0
GRID_W = 64
POOL_WINDOWS = (2, 4, 8, 16)
POOL_GROUP = 128
POOL_WIDTH = len(POOL_WINDOWS) * POOL_GROUP
MLA_WIDTH = MLA_HEADS * V_DIM
CHUNK = 128
SGU_GROUPS = 4
SGU_GROUP_DIM = D_MODEL // SGU_GROUPS
LOG2_E = 1.4426950408889634
Q_SCALE = (QK_NOPE + QK_ROPE) ** -0.5 * LOG2_E

QK_PAD = 256
POOL_HALO = 8
MOD_ROWS = 16
VMEM_LIMIT = 56 * 1024 * 1024

_C_QLAT = 0
_C_KVLAT = _C_QLAT + Q_LORA
_C_GATE_A = _C_KVLAT + KV_LORA
_C_POOL = _C_GATE_A + MLA_WIDTH
_C_GATE_B = _C_POOL + POOL_WIDTH
_C_KPE = _C_GATE_B + POOL_WIDTH
_C_END = _C_KPE + 2 * QK_ROPE

BF16 = jnp.bfloat16
F32 = jnp.float32


def _dot(a, b):
    return jnp.dot(a, b, preferred_element_type=F32)


def _dot_nt(a, b):
    return lax.dot_general(a, b, (((1,), (1,)), ((), ())), preferred_element_type=F32)


def _rms(x, g):
    return x * lax.rsqrt(jnp.mean(x * x, axis=-1, keepdims=True) + EPS) * g


def _ada_kernel(c_ref, w_ref, b_ref, o_ref):
    a = jax.nn.silu(c_ref[...]).astype(BF16)
    o_ref[0] = _dot(a, w_ref[0].astype(BF16)) + b_ref[0]


def _ada(cond, w_ada, b_ada):
    depth, d, n = w_ada.shape
    bn = 512
    return pl.pallas_call(
        _ada_kernel,
        grid=(depth, n // bn),
        in_specs=[
            pl.BlockSpec((MOD_ROWS, d), lambda l, j: (0, 0)),
            pl.BlockSpec((1, d, bn), lambda l, j: (l, 0, j)),
            pl.BlockSpec((1, 1, bn), lambda l, j: (l, 0, j)),
        ],
        out_specs=pl.BlockSpec((1, MOD_ROWS, bn), lambda l, j: (l, 0, j)),
        out_shape=jax.ShapeDtypeStruct((depth, MOD_ROWS, n), F32),
        name="ada_mod",
    )(cond, w_ada, b_ada.reshape(depth, 1, n))


def _front_kernel(*refs, tm, seq, halo, rope, emit_cache):
    it = iter(refs)
    x_ref = next(it)
    xp_ref = next(it) if halo else None
    xn_ref = next(it) if halo else None
    mod_ref = next(it)
    npre_ref = next(it)
    win_ref = next(it)
    qn_ref = next(it)
    wuq_ref = next(it)
    kvn_ref = next(it)
    wuk_ref = next(it)
    wuvt_ref = next(it)
    wpool_ref = next(it)
    pscale_ref = next(it)
    cos_ref = next(it) if rope else None
    sin_ref = next(it) if rope else None
    q_ref = next(it)
    k_ref = next(it)
    vt_ref = next(it)
    ga_ref = next(it)
    pp_ref = next(it)
    ckv_ref = next(it) if emit_cache else None
    kpe_ref = next(it) if emit_cache else None

    i = pl.program_id(0)
    nt = pl.num_programs(0)
    d = D_MODEL
    shift = mod_ref[0, :, 0:d]
    scale = mod_ref[0, :, d:2 * d]
    npre = npre_ref[...]

    def modulate(xv):
        return (_rms(xv, npre) * (1.0 + scale) + shift).astype(BF16)

    def rotate(blk):
        if not rope:
            return blk
        return blk * cos_ref[...] + pltpu.roll(blk, QK_ROPE, axis=1) * sin_ref[...]

    h = modulate(x_ref[0])

    q_lat = _dot(h, win_ref[:, _C_QLAT:_C_KVLAT])
    qn = _rms(q_lat, qn_ref[...]).astype(BF16)
    for hd in range(MLA_HEADS):
        base = hd * QK_PAD
        q_ref[0, :, base:base + QK_NOPE] = (
            _dot(qn, wuq_ref[:, base:base + QK_NOPE]) * Q_SCALE).astype(BF16)
        qpe = _dot(qn, wuq_ref[:, base + QK_NOPE:base + QK_PAD]) * Q_SCALE
        q_ref[0, :, base + QK_NOPE:base + QK_PAD] = rotate(qpe).astype(BF16)

    ckv = _rms(_dot(h, win_ref[:, _C_KVLAT:_C_GATE_A]), kvn_ref[...])
    kpe2 = _dot(h, win_ref[:, _C_KPE:_C_END])
    if emit_cache:
        ckv_ref[0] = ckv
        kpe_ref[0] = kpe2[:, 0:QK_ROPE]
    ckv_b = ckv.astype(BF16)
    kpe_b = rotate(kpe2).astype(BF16)
    vt = _dot_nt(wuvt_ref[...], ckv_b)
    for hd in range(MLA_HEADS):
        k_ref[0, hd, :, 0:QK_NOPE] = _dot(
            ckv_b, wuk_ref[:, hd * QK_NOPE:(hd + 1) * QK_NOPE]).astype(BF16)
        k_ref[0, hd, :, QK_NOPE:QK_PAD] = kpe_b
        vt_ref[0, hd, 0] = vt[hd * V_DIM:(hd + 1) * V_DIM].astype(BF16)

    ga_ref[0] = jax.nn.silu(_dot(h, win_ref[:, _C_GATE_A:_C_POOL])).astype(BF16)

    u = _dot(h, win_ref[:, _C_POOL:_C_GATE_B])
    if halo:
        hh = modulate(jnp.concatenate([xp_ref[0], xn_ref[0]], axis=0))
        uh = _dot(hh, win_ref[:, _C_POOL:_C_GATE_B])
        up = jnp.where(i > 0, uh[0:POOL_HALO], 0.0)
        un = jnp.where(i < nt - 1, uh[POOL_HALO:2 * POOL_HALO], 0.0)
    else:
        up = jnp.zeros((POOL_HALO, POOL_WIDTH), F32)
        un = up
    ue = jnp.concatenate([up, u, un], axis=0)
    ext = tm + 2 * POOL_HALO
    t = i * tm + lax.broadcasted_iota(jnp.int32, (tm, 1), 0)
    gate_b = _dot(h, win_ref[:, _C_GATE_B:_C_KPE])
    for g, w in enumerate(POOL_WINDOWS):
        sl = slice(g * POOL_GROUP, (g + 1) * POOL_GROUP)
        p = ue[:, sl]
        k = 1
        while k < w:
            p = p + pltpu.roll(p, k, axis=0)
            k *= 2
        lead = w // 2 - 1
        if lead:
            p = pltpu.roll(p, ext - lead, axis=0)
        wsum = p[POOL_HALO:POOL_HALO + tm]
        cnt = (jnp.minimum(t + w // 2, seq) - jnp.maximum(t - w // 2, 0)).astype(F32)
        dlt = (wsum * (1.0 / cnt) - u[:, sl]).astype(BF16)
        og = _dot(dlt, wpool_ref[g]) * pscale_ref[:, sl]
        pp_ref[0, :, sl] = (og * jax.nn.silu(gate_b[:, sl])).astype(BF16)


def _front(x, mod, mod_row, npre, w_in, qn, w_uq, kvn, w_uk, w_uvt, w_pool, pscale, tables, *, tm, emit_cache):
    b, seq, d = x.shape
    nt = seq // tm
    halo = nt > 1
    rope = tables is not None
    hb = tm // POOL_HALO
    last = seq // POOL_HALO - 1

    def const(shape):
        return pl.BlockSpec(shape, lambda i, j: (0,) * len(shape))

    in_specs = [pl.BlockSpec((1, tm, d), lambda i, j: (j, i, 0))]
    args = [x]
    if halo:
        in_specs += [
            pl.BlockSpec((1, POOL_HALO, d), lambda i, j: (j, jnp.maximum(i * hb - 1, 0), 0)),
            pl.BlockSpec((1, POOL_HALO, d), lambda i, j: (j, jnp.minimum((i + 1) * hb, last), 0)),
        ]
        args += [x, x]
    in_specs += [
        pl.BlockSpec((1, 1, 3 * d), lambda i, j: (mod_row(j), 0, 0)),
        const((1, d)), const(w_in.shape), const((1, Q_LORA)), const(w_uq.shape),
        const((1, KV_LORA)), const(w_uk.shape), const(w_uvt.shape), const(w_pool.shape),
        const((1, POOL_WIDTH)),
    ]
    args += [mod, npre, w_in, qn, w_uq, kvn, w_uk, w_uvt, w_pool, pscale]
    if rope:
        in_specs += [pl.BlockSpec((tm, 2 * QK_ROPE), lambda i, j: (i, 0))] * 2
        args += list(tables)

    out_specs = [
        pl.BlockSpec((1, tm, MLA_HEADS * QK_PAD), lambda i, j: (j, i, 0)),
        pl.BlockSpec((1, MLA_HEADS, tm, QK_PAD), lambda i, j: (j, 0, i, 0)),
        pl.BlockSpec((1, MLA_HEADS, 1, V_DIM, tm), lambda i, j: (j, 0, i, 0, 0)),
        pl.BlockSpec((1, tm, MLA_WIDTH), lambda i, j: (j, i, 0)),
        pl.BlockSpec((1, tm, POOL_WIDTH), lambda i, j: (j, i, 0)),
    ]
    out_shape = [
        jax.ShapeDtypeStruct((b, seq, MLA_HEADS * QK_PAD), BF16),
        jax.ShapeDtypeStruct((b, MLA_HEADS, seq, QK_PAD), BF16),
        jax.ShapeDtypeStruct((b, MLA_HEADS, nt, V_DIM, tm), BF16),
        jax.ShapeDtypeStruct((b, seq, MLA_WIDTH), BF16),
        jax.ShapeDtypeStruct((b, seq, POOL_WIDTH), BF16),
    ]
    if emit_cache:
        out_specs += [
            pl.BlockSpec((1, tm, KV_LORA), lambda i, j: (j, i, 0)),
            pl.BlockSpec((1, tm, QK_ROPE), lambda i, j: (j, i, 0)),
        ]
        out_shape += [
            jax.ShapeDtypeStruct((b, seq, KV_LORA), F32),
            jax.ShapeDtypeStruct((b, seq, QK_ROPE), F32),
        ]
    return pl.pallas_call(
        functools.partial(_front_kernel, tm=tm, seq=seq, halo=halo, rope=rope, emit_cache=emit_cache),
        grid=(nt, b),
        in_specs=in_specs,
        out_specs=out_specs,
        out_shape=out_shape,
        compiler_params=pltpu.CompilerParams(
            dimension_semantics=("arbitrary", "arbitrary"), vmem_limit_bytes=VMEM_LIMIT),
        name="mla_pool_front",
    )(*args)


def _ctx_kv_kernel(ckv_ref, kpe_ref, wuk_ref, wuvt_ref, k_ref, vt_ref):
    ckv_b = ckv_ref[0].astype(BF16)
    kpe_b = kpe_ref[0].astype(BF16)
    vt = _dot_nt(wuvt_ref[...], ckv_b)
    for hd in range(MLA_HEADS):
        k_ref[0, hd, :, 0:QK_NOPE] = _dot(
            ckv_b, wuk_ref[:, hd * QK_NOPE:(hd + 1) * QK_NOPE]).astype(BF16)
        k_ref[0, hd, :, QK_NOPE:QK_PAD] = kpe_b
        vt_ref[0, hd] = vt[hd * V_DIM:(hd + 1) * V_DIM].astype(BF16)


def _ctx_kv(ckv, kpe_pad, w_uk, w_uvt):
    b, past, _ = ckv.shape
    return pl.pallas_call(
        _ctx_kv_kernel,
        grid=(b,),
        in_specs=[
            pl.BlockSpec((1, past, KV_LORA), lambda j: (j, 0, 0)),
            pl.BlockSpec((1, past, 2 * QK_ROPE), lambda j: (j, 0, 0)),
            pl.BlockSpec(w_uk.shape, lambda j: (0, 0)),
            pl.BlockSpec(w_uvt.shape, lambda j: (0, 0)),
        ],
        out_specs=[
            pl.BlockSpec((1, MLA_HEADS, past, QK_PAD), lambda j: (j, 0, 0, 0)),
            pl.BlockSpec((1, MLA_HEADS, V_DIM, past), lambda j: (j, 0, 0, 0)),
        ],
        out_shape=[
            jax.ShapeDtypeStruct((b, MLA_HEADS, past, QK_PAD), BF16),
            jax.ShapeDtypeStruct((b, MLA_HEADS, V_DIM, past), BF16),
        ],
        name="ctx_kv",
    )(ckv, kpe_pad, w_uk, w_uvt)


def _attn_kernel(*refs, tq, kc, nchunks, has_ctx):
    it = iter(refs)
    q_ref = next(it)
    k_ref = next(it)
    vt_ref = next(it)
    kc_ref = next(it) if has_ctx else None
    vtc_ref = next(it) if has_ctx else None
    ga_ref = next(it)
    pp_ref = next(it)
    x_ref = next(it)
    mod_ref = next(it)
    npost_ref = next(it)
    wout_ref = next(it)
    o_ref = next(it)
    s_scr = next(it)
    mix_scr = next(it)

    d = D_MODEL
    sub = 8

    for hd in range(MLA_HEADS):
        qh = q_ref[0, :, hd * QK_PAD:(hd + 1) * QK_PAD]
        s_buf = s_scr.at[hd % 2]
        m8 = None
        for c in range(nchunks):
            s = _dot_nt(k_ref[0, hd, c * kc:(c + 1) * kc, :], qh)
            s_buf[c * kc:(c + 1) * kc, :] = s
            cm = jnp.max(s.reshape(kc // sub, sub, tq), axis=0)
            m8 = cm if m8 is None else jnp.maximum(m8, cm)
        m = jnp.max(m8, axis=0, keepdims=True)
        if has_ctx:
            sc = _dot_nt(kc_ref[0, hd], qh)
            m = jnp.maximum(m, jnp.max(sc, axis=0, keepdims=True))

        l8 = jnp.zeros((sub, tq), F32)
        acc = jnp.zeros((V_DIM, tq), F32)
        for c in range(nchunks):
            p = jnp.exp2(s_buf[c * kc:(c + 1) * kc, :] - m)
            l8 = l8 + jnp.sum(p.reshape(kc // sub, sub, tq), axis=0)
            acc = acc + _dot(vt_ref[0, hd, c], p.astype(BF16))
        l = jnp.sum(l8, axis=0, keepdims=True)
        if has_ctx:
            pc = jnp.exp2(sc - m)
            l = l + jnp.sum(pc, axis=0, keepdims=True)
            acc = acc + _dot(vtc_ref[0, hd], pc.astype(BF16))
        o = (acc * (1.0 / l)).T
        sl = slice(hd * V_DIM, (hd + 1) * V_DIM)
        mix_scr[:, sl] = (o * ga_ref[0, :, sl].astype(F32)).astype(BF16)
    mix_scr[:, MLA_WIDTH:] = pp_ref[0]

    out = _dot(mix_scr[...], wout_ref[...])
    gate = mod_ref[0, :, 2 * d:3 * d]
    o_ref[0] = x_ref[0] + gate * _rms(out, npost_ref[...])


def _attend(q, k, vt, ctx, ga, pp, x, mod, mod_row, npost, w_out, *, tq):
    b, seq, d = x.shape
    nchunks, kc = vt.shape[2], vt.shape[4]
    has_ctx = ctx is not None

    def const(shape):
        return pl.BlockSpec(shape, lambda j, i: (0,) * len(shape))

    in_specs = [
        pl.BlockSpec((1, tq, MLA_HEADS * QK_PAD), lambda j, i: (j, i, 0)),
        pl.BlockSpec((1, MLA_HEADS, seq, QK_PAD), lambda j, i: (j, 0, 0, 0)),
        pl.BlockSpec((1, MLA_HEADS, nchunks, V_DIM, kc), lambda j, i: (j, 0, 0, 0, 0)),
    ]
    args = [q, k, vt]
    if has_ctx:
        kctx, vtctx = ctx
        past = kctx.shape[2]
        in_specs += [
            pl.BlockSpec((1, MLA_HEADS, past, QK_PAD), lambda j, i: (j, 0, 0, 0)),
            pl.BlockSpec((1, MLA_HEADS, V_DIM, past), lambda j, i: (j, 0, 0, 0)),
        ]
        args += [kctx, vtctx]
    in_specs += [
        pl.BlockSpec((1, tq, MLA_WIDTH), lambda j, i: (j, i, 0)),
        pl.BlockSpec((1, tq, POOL_WIDTH), lambda j, i: (j, i, 0)),
        pl.BlockSpec((1, tq, d), lambda j, i: (j, i, 0)),
        pl.BlockSpec((1, 1, 3 * d), lambda j, i: (mod_row(j), 0, 0)),
        const((1, d)), const(w_out.shape),
    ]
    args += [ga, pp, x, mod, npost, w_out]
    return pl.pallas_call(
        functools.partial(_attn_kernel, tq=tq, kc=kc, nchunks=nchunks, has_ctx=has_ctx),
        grid=(b, seq // tq),
        in_specs=in_specs,
        out_specs=pl.BlockSpec((1, tq, d), lambda j, i: (j, i, 0)),
        out_shape=jax.ShapeDtypeStruct((b, seq, d), F32),
        scratch_shapes=[
            pltpu.VMEM((2, seq, tq), F32),
            pltpu.VMEM((tq, MLA_WIDTH + POOL_WIDTH), BF16),
        ],
        compiler_params=pltpu.CompilerParams(
            dimension_semantics=("arbitrary", "arbitrary"), vmem_limit_bytes=VMEM_LIMIT),
        name="mla_attend_out",
    )(*args)


def _gmlp_kernel(x_ref, mod_ref, npre_ref, win_ref, lng_ref, lnb_ref, ws_ref, bs_ref, wout_ref,
                 npost_ref, o_ref, z_scr, *, tm):
    d = D_MODEL
    x = x_ref[0]
    shift = mod_ref[0, :, 0:d]
    scale = mod_ref[0, :, d:2 * d]
    gate = mod_ref[0, :, 2 * d:3 * d]
    h = (_rms(x, npre_ref[...]) * (1.0 + scale) + shift).astype(BF16)
    u = jax.nn.gelu(_dot(h, win_ref[:, 0:d]))
    v = jax.nn.gelu(_dot(h, win_ref[:, d:2 * d]))
    sg = jax.nn.silu(_dot(h, win_ref[:, 2 * d:3 * d]))
    mu = jnp.mean(v, axis=-1, keepdims=True)
    vc = v - mu
    var = jnp.mean(vc * vc, axis=-1, keepdims=True)
    vn = (vc * lax.rsqrt(var + EPS) * lng_ref[...] + lnb_ref[...]).astype(BF16)
    us = u * sg
    for n in range(tm // CHUNK):
        rows = slice(n * CHUNK, (n + 1) * CHUNK)
        for g in range(SGU_GROUPS):
            cols = slice(g * SGU_GROUP_DIM, (g + 1) * SGU_GROUP_DIM)
            sv = _dot(ws_ref[g], vn[rows, cols]) + bs_ref[:, cols]
            z_scr[rows, cols] = (us[rows, cols] * sv).astype(BF16)
    out = _dot(z_scr[...], wout_ref[...])
    o_ref[0] = x + gate * _rms(out, npost_ref[...])


def _gmlp(x, mod, mod_row, npre, w_in, lng, lnb, w_s, bias, w_out, npost, *, tm):
    b, seq, d = x.shape

    def const(shape):
        return pl.BlockSpec(shape, lambda j, i: (0,) * len(shape))

    return pl.pallas_call(
        functools.partial(_gmlp_kernel, tm=tm),
        grid=(b, seq // tm),
        in_specs=[
            pl.BlockSpec((1, tm, d), lambda j, i: (j, i, 0)),
            pl.BlockSpec((1, 1, 3 * d), lambda j, i: (mod_row(j), 0, 0)),
            const((1, d)), const(w_in.shape), const((1, d)), const((1, d)),
            const(w_s.shape), const(bias.shape), const(w_out.shape), const((1, d)),
        ],
        out_specs=pl.BlockSpec((1, tm, d), lambda j, i: (j, i, 0)),
        out_shape=jax.ShapeDtypeStruct((b, seq, d), F32),
        scratch_shapes=[pltpu.VMEM((tm, d), BF16)],
        compiler_params=pltpu.CompilerParams(
            dimension_semantics=("arbitrary", "arbitrary"), vmem_limit_bytes=VMEM_LIMIT),
        name="gmlp_layer",
    )(x, mod, npre, w_in, lng, lnb, w_s, bias, w_out, npost)


def _swap16(w):
    half = QK_ROPE // 4
    parts = [w[..., k * half:(k + 1) * half] for k in range(4)]
    return jnp.concatenate([parts[1], parts[0], parts[3], parts[2]], axis=-1)


def _rope_tables(seq):
    t = jnp.arange(seq)
    half = QK_ROPE // 4
    inv = ROPE_THETA ** (-jnp.arange(half, dtype=F32) / half)
    ang_r = (t // GRID_W).astype(F32)[:, None] * inv
    ang_c = (t % GRID_W).astype(F32)[:, None] * inv
    cr, sr, cc, sc = jnp.cos(ang_r), jnp.sin(ang_r), jnp.cos(ang_c), jnp.sin(ang_c)
    zero = jnp.zeros((seq, QK_ROPE), F32)
    cos = jnp.concatenate([cr, cr, cc, cc, zero], axis=-1)
    sin = jnp.concatenate([-sr, sr, -sc, sc, zero], axis=-1)
    return cos, sin


def kernel(x_prompt, x_sample, cache_ckv, cache_kpe, c, c_ctx, w_ada, b_ada, norm_pre, norm_post,
           w_in_ap, q_norm, w_uq, kv_norm, w_ukv, w_pool, pool_scale, w_out_ap,
           w_in_c, sgu_ln_g, sgu_ln_b, w_s, b_s, w_out_c):
    d = D_MODEL
    dec_b = x_sample.shape[0]
    ctx_row = dec_b

    cond = jnp.zeros((MOD_ROWS, d), F32).at[:dec_b].set(c).at[ctx_row].set(c_ctx)
    mod = _ada(cond, w_ada, b_ada)
    mod0 = mod[0].reshape(MOD_ROWS, 1, 3 * d)
    mod1 = mod[1].reshape(MOD_ROWS, 1, 3 * d)
    row_sample = lambda j: j
    row_prompt = lambda j: ctx_row

    wi = w_in_ap[0]
    q_lat, kv_lat, k_pe, gate_a, pool_in, gate_b = jnp.split(
        wi, (Q_LORA, Q_LORA + KV_LORA, Q_LORA + KV_LORA + QK_ROPE,
             Q_LORA + KV_LORA + QK_ROPE + MLA_WIDTH,
             Q_LORA + KV_LORA + QK_ROPE + MLA_WIDTH + POOL_WIDTH), axis=1)
    body = [q_lat, kv_lat, gate_a, pool_in, gate_b, k_pe]
    w_in_rope = jnp.concatenate(body + [_swap16(k_pe)], axis=1).astype(BF16)
    w_in_plain = jnp.concatenate(body + [jnp.zeros_like(k_pe)], axis=1).astype(BF16)

    wq = w_uq[0].reshape(Q_LORA, MLA_HEADS, QK_NOPE + QK_ROPE)
    wq_n, wq_p = wq[..., :QK_NOPE], wq[..., QK_NOPE:]
    w_uq_rope = jnp.concatenate([wq_n, wq_p, _swap16(wq_p)], axis=-1).reshape(Q_LORA, -1).astype(BF16)
    w_uq_plain = jnp.concatenate([wq_n, wq_p, jnp.zeros_like(wq_p)], axis=-1).reshape(Q_LORA, -1).astype(BF16)

    wkv = w_ukv[0].reshape(KV_LORA, MLA_HEADS, QK_NOPE + V_DIM)
    w_uk = wkv[..., :QK_NOPE].reshape(KV_LORA, MLA_HEADS * QK_NOPE).astype(BF16)
    w_uvt = wkv[..., QK_NOPE:].transpose(1, 2, 0).reshape(MLA_HEADS * V_DIM, KV_LORA).astype(BF16)

    w_pool_b = w_pool[0].astype(BF16)
    w_out_b = w_out_ap[0].astype(BF16)
    npre0, npost0 = norm_pre[0][None], norm_post[0][None]
    npre1, npost1 = norm_pre[1][None], norm_post[1][None]
    qn, kvn, pscale = q_norm[0][None], kv_norm[0][None], pool_scale[0][None]

    w_in_c_b = w_in_c[0].astype(BF16)
    w_s_b = w_s[0].astype(BF16)
    bias = jnp.repeat(b_s[0].T, SGU_GROUP_DIM, axis=1)
    w_out_c_b = w_out_c[0].astype(BF16)
    lng, lnb = sgu_ln_g[0][None], sgu_ln_b[0][None]

    seq_p = x_prompt.shape[1]
    qp, kp, vtp, gap, ppp, ckv_new, kpe_new = _front(
        x_prompt, mod0, row_prompt, npre0, w_in_plain, qn, w_uq_plain, kvn, w_uk, w_uvt,
        w_pool_b, pscale, None, tm=seq_p, emit_cache=True)
    xp1 = _attend(qp, kp, vtp, None, gap, ppp, x_prompt, mod0, row_prompt, npost0, w_out_b, tq=seq_p)
    y_prompt = _gmlp(xp1, mod1, row_prompt, npre1, w_in_c_b, lng, lnb, w_s_b, bias, w_out_c_b, npost1,
                     tm=seq_p)

    seq_s = x_sample.shape[1]
    kpe_pad = jnp.pad(cache_kpe[:, 0], ((0, 0), (0, 0), (0, QK_ROPE)))
    ctx = _ctx_kv(cache_ckv[:, 0], kpe_pad, w_uk, w_uvt)
    qs, ks, vts, gas, pps = _front(
        x_sample, mod0, row_sample, npre0, w_in_rope, qn, w_uq_rope, kvn, w_uk, w_uvt,
        w_pool_b, pscale, _rope_tables(seq_s), tm=512, emit_cache=False)
    xs1 = _attend(qs, ks, vts, ctx, gas, pps, x_sample, mod0, row_sample, npost0, w_out_b, tq=256)
    y_sample = _gmlp(xs1, mod1, row_sample, npre1, w_in_c_b, lng, lnb, w_s_b, bias, w_out_c_b, npost1,
                     tm=512)

    return (y_prompt, y_sample, ckv_new[:, None], kpe_new[:, None])
```

```python
import functools

import jax
import jax.numpy as jnp
from jax import lax
from jax.experimental import pallas as pl
from jax.experimental.pallas import tpu as pltpu

D_MODEL = 1024
EPS = 1e-6
MLA_HEADS = 4
Q_LORA = 256
KV_LORA = 128
QK_NOPE = 128
QK_ROPE = 64
V_DIM = 128
ROPE_THETA = 10000.0
GRID_W = 64
POOL_WINDOWS = (2, 4, 8, 16)
POOL_GROUP = 128
POOL_WIDTH = len(POOL_WINDOWS) * POOL_GROUP
MLA_WIDTH = MLA_HEADS * V_DIM
CHUNK = 128
SGU_GROUPS = 4
SGU_GROUP_DIM = D_MODEL // SGU_GROUPS
LOG2_E = 1.4426950408889634
Q_SCALE = (QK_NOPE + QK_ROPE) ** -0.5 * LOG2_E

QK_PAD = KV_LORA + 2 * QK_ROPE
BF16_SUBLANES = 16
LAT_ROWS = KV_LORA + BF16_SUBLANES
KEY_BLOCK = 256
POOL_HALO = 8
MOD_ROWS = 16
VMEM_LIMIT = 56 * 1024 * 1024

_C_QLAT = 0
_C_KVLAT = _C_QLAT + Q_LORA
_C_GATE_A = _C_KVLAT + KV_LORA
_C_POOL = _C_GATE_A + MLA_WIDTH
_C_GATE_B = _C_POOL + POOL_WIDTH
_C_KPE = _C_GATE_B + POOL_WIDTH
_C_END = _C_KPE + 2 * QK_ROPE

BF16 = jnp.bfloat16
F32 = jnp.float32


def _dot(a, b):
    return jnp.dot(a, b, preferred_element_type=F32)


def _dot_nt(a, b):
    return lax.dot_general(a, b, (((1,), (1,)), ((), ())), preferred_element_type=F32)


def _rms(x, g):
    return x * lax.rsqrt(jnp.mean(x * x, axis=-1, keepdims=True) + EPS) * g


def _ada_kernel(c_ref, w_ref, b_ref, o_ref):
    a = jax.nn.silu(c_ref[...]).astype(BF16)
    o_ref[0] = _dot(a, w_ref[0].astype(BF16)) + b_ref[0]


def _ada(cond, w_ada, b_ada):
    depth, d, n = w_ada.shape
    bn = 512
    return pl.pallas_call(
        _ada_kernel,
        grid=(depth, n // bn),
        in_specs=[
            pl.BlockSpec((MOD_ROWS, d), lambda l, j: (0, 0)),
            pl.BlockSpec((1, d, bn), lambda l, j: (l, 0, j)),
            pl.BlockSpec((1, 1, bn), lambda l, j: (l, 0, j)),
        ],
        out_specs=pl.BlockSpec((1, MOD_ROWS, bn), lambda l, j: (l, 0, j)),
        out_shape=jax.ShapeDtypeStruct((depth, MOD_ROWS, n), F32),
        name="ada_mod",
    )(cond, w_ada, b_ada.reshape(depth, 1, n))


def _front_kernel(*refs, tm, seq, halo, rope, emit_cache):
    it = iter(refs)
    x_ref = next(it)
    xp_ref = next(it) if halo else None
    xn_ref = next(it) if halo else None
    mod_ref = next(it)
    npre_ref = next(it)
    win_ref = next(it)
    qn_ref = next(it)
    wuq_ref = next(it)
    wukt_ref = next(it)
    kvn_ref = next(it)
    wpool_ref = next(it)
    pscale_ref = next(it)
    cos_ref = next(it) if rope else None
    sin_ref = next(it) if rope else None
    q_ref = next(it)
    k_ref = next(it)
    lt_ref = next(it)
    ga_ref = next(it)
    pp_ref = next(it)
    ckv_ref = next(it) if emit_cache else None
    kpe_ref = next(it) if emit_cache else None

    i = pl.program_id(0)
    nt = pl.num_programs(0)
    d = D_MODEL
    shift = mod_ref[0, :, 0:d]
    scale = mod_ref[0, :, d:2 * d]
    npre = npre_ref[...]

    def modulate(xv):
        return (_rms(xv, npre) * (1.0 + scale) + shift).astype(BF16)

    def rotate(blk):
        if not rope:
            return blk
        return blk * cos_ref[...] + pltpu.roll(blk, QK_ROPE, axis=1) * sin_ref[...]

    h = modulate(x_ref[0])

    q_lat = _dot(h, win_ref[:, _C_QLAT:_C_KVLAT])
    qn = _rms(q_lat, qn_ref[...]).astype(BF16)
    wq = QK_NOPE + 2 * QK_ROPE
    for hd in range(MLA_HEADS):
        base = hd * wq
        q_nope = _dot(qn, wuq_ref[:, base:base + QK_NOPE]).astype(BF16)
        q_ref[0, hd, :, 0:KV_LORA] = (_dot(q_nope, wukt_ref[hd]) * Q_SCALE).astype(BF16)
        qpe = _dot(qn, wuq_ref[:, base + QK_NOPE:base + wq]) * Q_SCALE
        q_ref[0, hd, :, KV_LORA:QK_PAD] = rotate(qpe).astype(BF16)

    ckv = _rms(_dot(h, win_ref[:, _C_KVLAT:_C_GATE_A]), kvn_ref[...])
    kpe2 = _dot(h, win_ref[:, _C_KPE:_C_END])
    if emit_cache:
        ckv_ref[0] = ckv
        kpe_ref[0] = kpe2[:, 0:QK_ROPE]
    k_ref[0, :, 0:KV_LORA] = ckv.astype(BF16)
    k_ref[0, :, KV_LORA:QK_PAD] = rotate(kpe2).astype(BF16)
    lt_ref[0, 0, 0:KV_LORA] = ckv.T.astype(BF16)
    lt_ref[0, 0, KV_LORA:LAT_ROWS] = jnp.ones((BF16_SUBLANES, tm), BF16)

    ga_ref[0] = jax.nn.silu(_dot(h, win_ref[:, _C_GATE_A:_C_POOL])).astype(BF16)

    u = _dot(h, win_ref[:, _C_POOL:_C_GATE_B])
    if halo:
        hh = modulate(jnp.concatenate([xp_ref[0], xn_ref[0]], axis=0))
        uh = _dot(hh, win_ref[:, _C_POOL:_C_GATE_B])
        up = jnp.where(i > 0, uh[0:POOL_HALO], 0.0)
        un = jnp.where(i < nt - 1, uh[POOL_HALO:2 * POOL_HALO], 0.0)
    else:
        up = jnp.zeros((POOL_HALO, POOL_WIDTH), F32)
        un = up
    ue = jnp.concatenate([up, u, un], axis=0)
    ext = tm + 2 * POOL_HALO
    t = i * tm + lax.broadcasted_iota(jnp.int32, (tm, 1), 0)
    gate_b = _dot(h, win_ref[:, _C_GATE_B:_C_KPE])
    for g, w in enumerate(POOL_WINDOWS):
        sl = slice(g * POOL_GROUP, (g + 1) * POOL_GROUP)
        p = ue[:, sl]
        k = 1
        while k < w:
            p = p + pltpu.roll(p, k, axis=0)
            k *= 2
        lead = w // 2 - 1
        if lead:
            p = pltpu.roll(p, ext - lead, axis=0)
        wsum = p[POOL_HALO:POOL_HALO + tm]
        cnt = (jnp.minimum(t + w // 2, seq) - jnp.maximum(t - w // 2, 0)).astype(F32)
        dlt = (wsum * (1.0 / cnt) - u[:, sl]).astype(BF16)
        og = _dot(dlt, wpool_ref[g]) * pscale_ref[:, sl]
        pp_ref[0, :, sl] = (og * jax.nn.silu(gate_b[:, sl])).astype(BF16)


def _front(x, mod, mod_row, npre, w_in, qn, w_uq, w_ukt, kvn, w_pool, pscale, tables, *, tm, emit_cache):
    b, seq, d = x.shape
    nt = seq // tm
    halo = nt > 1
    rope = tables is not None
    hb = tm // POOL_HALO
    last = seq // POOL_HALO - 1

    def const(shape):
        return pl.BlockSpec(shape, lambda i, j: (0,) * len(shape))

    in_specs = [pl.BlockSpec((1, tm, d), lambda i, j: (j, i, 0))]
    args = [x]
    if halo:
        in_specs += [
            pl.BlockSpec((1, POOL_HALO, d), lambda i, j: (j, jnp.maximum(i * hb - 1, 0), 0)),
            pl.BlockSpec((1, POOL_HALO, d), lambda i, j: (j, jnp.minimum((i + 1) * hb, last), 0)),
        ]
        args += [x, x]
    in_specs += [
        pl.BlockSpec((1, 1, 3 * d), lambda i, j: (mod_row(j), 0, 0)),
        const((1, d)), const(w_in.shape), const((1, Q_LORA)), const(w_uq.shape), const(w_ukt.shape),
        const((1, KV_LORA)), const(w_pool.shape), const((1, POOL_WIDTH)),
    ]
    args += [mod, npre, w_in, qn, w_uq, w_ukt, kvn, w_pool, pscale]
    if rope:
        in_specs += [pl.BlockSpec((tm, 2 * QK_ROPE), lambda i, j: (i, 0))] * 2
        args += list(tables)

    out_specs = [
        pl.BlockSpec((1, MLA_HEADS, tm, QK_PAD), lambda i, j: (j, 0, i, 0)),
        pl.BlockSpec((1, tm, QK_PAD), lambda i, j: (j, i, 0)),
        pl.BlockSpec((1, 1, LAT_ROWS, tm), lambda i, j: (j, i, 0, 0)),
        pl.BlockSpec((1, tm, MLA_WIDTH), lambda i, j: (j, i, 0)),
        pl.BlockSpec((1, tm, POOL_WIDTH), lambda i, j: (j, i, 0)),
    ]
    out_shape = [
        jax.ShapeDtypeStruct((b, MLA_HEADS, seq, QK_PAD), BF16),
        jax.ShapeDtypeStruct((b, seq, QK_PAD), BF16),
        jax.ShapeDtypeStruct((b, nt, LAT_ROWS, tm), BF16),
        jax.ShapeDtypeStruct((b, seq, MLA_WIDTH), BF16),
        jax.ShapeDtypeStruct((b, seq, POOL_WIDTH), BF16),
    ]
    if emit_cache:
        out_specs += [
            pl.BlockSpec((1, tm, KV_LORA), lambda i, j: (j, i, 0)),
            pl.BlockSpec((1, tm, QK_ROPE), lambda i, j: (j, i, 0)),
        ]
        out_shape += [
            jax.ShapeDtypeStruct((b, seq, KV_LORA), F32),
            jax.ShapeDtypeStruct((b, seq, QK_ROPE), F32),
        ]
    return pl.pallas_call(
        functools.partial(_front_kernel, tm=tm, seq=seq, halo=halo, rope=rope, emit_cache=emit_cache),
        grid=(nt, b),
        in_specs=in_specs,
        out_specs=out_specs,
        out_shape=out_shape,
        compiler_params=pltpu.CompilerParams(
            dimension_semantics=("arbitrary", "arbitrary"), vmem_limit_bytes=VMEM_LIMIT),
        name="mla_pool_front",
    )(*args)


def _ctx_kernel(ckv_ref, kpe_ref, k_ref, lt_ref):
    ckv = ckv_ref[0]
    k_ref[0, :, 0:KV_LORA] = ckv.astype(BF16)
    k_ref[0, :, KV_LORA:QK_PAD] = kpe_ref[0].astype(BF16)
    lt_ref[0, 0:KV_LORA] = ckv.T.astype(BF16)
    lt_ref[0, KV_LORA:LAT_ROWS] = jnp.ones((BF16_SUBLANES, ckv.shape[0]), BF16)


def _ctx_keys(ckv, kpe_pad):
    b, past, _ = ckv.shape
    return pl.pallas_call(
        _ctx_kernel,
        grid=(b,),
        in_specs=[
            pl.BlockSpec((1, past, KV_LORA), lambda j: (j, 0, 0)),
            pl.BlockSpec((1, past, 2 * QK_ROPE), lambda j: (j, 0, 0)),
        ],
        out_specs=[
            pl.BlockSpec((1, past, QK_PAD), lambda j: (j, 0, 0)),
            pl.BlockSpec((1, LAT_ROWS, past), lambda j: (j, 0, 0)),
        ],
        out_shape=[
            jax.ShapeDtypeStruct((b, past, QK_PAD), BF16),
            jax.ShapeDtypeStruct((b, LAT_ROWS, past), BF16),
        ],
        name="ctx_keys",
    )(ckv, kpe_pad)


def _attn_kernel(*refs, tq, kc, nchunks, has_ctx):
    it = iter(refs)
    q_ref = next(it)
    k_ref = next(it)
    lt_ref = next(it)
    kc_ref = next(it) if has_ctx else None
    ltc_ref = next(it) if has_ctx else None
    wuvt_ref = next(it)
    ga_ref = next(it)
    pp_ref = next(it)
    x_ref = next(it)
    mod_ref = next(it)
    npost_ref = next(it)
    wout_ref = next(it)
    o_ref = next(it)
    sa_scr = next(it)
    sb_scr = next(it)
    mix_scr = next(it)

    d = D_MODEL
    sub = 8

    blocks = []
    for c in range(nchunks):
        for r in range(0, kc, KEY_BLOCK):
            blocks.append((c * kc + r, KEY_BLOCK,
                           k_ref.at[0, c * kc + r:c * kc + r + KEY_BLOCK, :],
                           lt_ref.at[0, c, :, r:r + KEY_BLOCK]))
    if has_ctx:
        blocks.append((nchunks * kc, kc_ref.shape[1], kc_ref.at[0], ltc_ref.at[0]))
    s_bufs = (sa_scr, sb_scr)

    def score_block(hd, bi, m8):
        r0, n, kblk, _ = blocks[bi]
        s = _dot_nt(kblk[...], q_ref[0, hd])
        s_bufs[hd % 2][r0:r0 + n, :] = s
        cm = jnp.max(s.reshape(n // sub, sub, tq), axis=0)
        return cm if m8 is None else jnp.maximum(m8, cm)

    m8 = None
    for bi in range(len(blocks)):
        m8 = score_block(0, bi, m8)
    for hd in range(MLA_HEADS):
        m = jnp.max(m8, axis=0, keepdims=True)
        m8 = None
        if hd + 1 < MLA_HEADS:
            for bi in range(len(blocks)):
                m8 = score_block(hd + 1, bi, m8)
        else:
            out = _dot(pp_ref[0], wout_ref[MLA_WIDTH:, :])
        acc = None
        for r0, n, _, ltblk in blocks:
            p = jnp.exp2(s_bufs[hd % 2][r0:r0 + n, :] - m).astype(BF16)
            part = _dot(ltblk[...], p)
            acc = part if acc is None else acc + part
        o_lat = (acc[0:KV_LORA] * (1.0 / acc[KV_LORA:KV_LORA + 1])).astype(BF16)
        o = _dot(wuvt_ref[hd], o_lat).T
        sl = slice(hd * V_DIM, (hd + 1) * V_DIM)
        mix_scr[:, sl] = (o * ga_ref[0, :, sl].astype(F32)).astype(BF16)

    out = out + _dot(mix_scr[...], wout_ref[0:MLA_WIDTH, :])
    gate = mod_ref[0, :, 2 * d:3 * d]
    o_ref[0] = x_ref[0] + gate * _rms(out, npost_ref[...])


def _attend(q, k, lt, ctx, w_uvt, ga, pp, x, mod, mod_row, npost, w_out, *, tq):
    b, seq, d = x.shape
    nchunks, kc = lt.shape[1], lt.shape[3]
    has_ctx = ctx is not None
    n_keys = seq + (ctx[0].shape[1] if has_ctx else 0)

    def const(shape):
        return pl.BlockSpec(shape, lambda j, i: (0,) * len(shape))

    in_specs = [
        pl.BlockSpec((1, MLA_HEADS, tq, QK_PAD), lambda j, i: (j, 0, i, 0)),
        pl.BlockSpec((1, seq, QK_PAD), lambda j, i: (j, 0, 0)),
        pl.BlockSpec((1, nchunks, LAT_ROWS, kc), lambda j, i: (j, 0, 0, 0)),
    ]
    args = [q, k, lt]
    if has_ctx:
        kctx, ltctx = ctx
        past = kctx.shape[1]
        in_specs += [
            pl.BlockSpec((1, past, QK_PAD), lambda j, i: (j, 0, 0)),
            pl.BlockSpec((1, LAT_ROWS, past), lambda j, i: (j, 0, 0)),
        ]
        args += [kctx, ltctx]
    in_specs += [
        const(w_uvt.shape),
        pl.BlockSpec((1, tq, MLA_WIDTH), lambda j, i: (j, i, 0)),
        pl.BlockSpec((1, tq, POOL_WIDTH), lambda j, i: (j, i, 0)),
        pl.BlockSpec((1, tq, d), lambda j, i: (j, i, 0)),
        pl.BlockSpec((1, 1, 3 * d), lambda j, i: (mod_row(j), 0, 0)),
        const((1, d)), const(w_out.shape),
    ]
    args += [w_uvt, ga, pp, x, mod, npost, w_out]
    return pl.pallas_call(
        functools.partial(_attn_kernel, tq=tq, kc=kc, nchunks=nchunks, has_ctx=has_ctx),
        grid=(b, seq // tq),
        in_specs=in_specs,
        out_specs=pl.BlockSpec((1, tq, d), lambda j, i: (j, i, 0)),
        out_shape=jax.ShapeDtypeStruct((b, seq, d), F32),
        scratch_shapes=[
            pltpu.VMEM((n_keys, tq), F32),
            pltpu.VMEM((n_keys, tq), F32),
            pltpu.VMEM((tq, MLA_WIDTH), BF16),
        ],
        compiler_params=pltpu.CompilerParams(
            dimension_semantics=("arbitrary", "arbitrary"), vmem_limit_bytes=VMEM_LIMIT),
        name="mla_attend_out",
    )(*args)


def _gmlp_kernel(x_ref, mod_ref, npre_ref, win_ref, lng_ref, lnb_ref, ws_ref, bs_ref, wout_ref,
                 npost_ref, o_ref, z_scr, *, tm):
    d = D_MODEL
    x = x_ref[0]
    shift = mod_ref[0, :, 0:d]
    scale = mod_ref[0, :, d:2 * d]
    gate = mod_ref[0, :, 2 * d:3 * d]
    h = (_rms(x, npre_ref[...]) * (1.0 + scale) + shift).astype(BF16)
    u = jax.nn.gelu(_dot(h, win_ref[:, 0:d]))
    v = jax.nn.gelu(_dot(h, win_ref[:, d:2 * d]))
    sg = jax.nn.silu(_dot(h, win_ref[:, 2 * d:3 * d]))
    mu = jnp.mean(v, axis=-1, keepdims=True)
    vc = v - mu
    var = jnp.mean(vc * vc, axis=-1, keepdims=True)
    vn = (vc * lax.rsqrt(var + EPS) * lng_ref[...] + lnb_ref[...]).astype(BF16)
    us = u * sg
    for n in range(tm // CHUNK):
        rows = slice(n * CHUNK, (n + 1) * CHUNK)
        for g in range(SGU_GROUPS):
            cols = slice(g * SGU_GROUP_DIM, (g + 1) * SGU_GROUP_DIM)
            sv = _dot(ws_ref[g], vn[rows, cols]) + bs_ref[:, cols]
            z_scr[rows, cols] = (us[rows, cols] * sv).astype(BF16)
    out = _dot(z_scr[...], wout_ref[...])
    o_ref[0] = x + gate * _rms(out, npost_ref[...])


def _gmlp(x, mod, mod_row, npre, w_in, lng, lnb, w_s, bias, w_out, npost, *, tm):
    b, seq, d = x.shape

    def const(shape):
        return pl.BlockSpec(shape, lambda j, i: (0,) * len(shape))

    return pl.pallas_call(
        functools.partial(_gmlp_kernel, tm=tm),
        grid=(b, seq // tm),
        in_specs=[
            pl.BlockSpec((1, tm, d), lambda j, i: (j, i, 0)),
            pl.BlockSpec((1, 1, 3 * d), lambda j, i: (mod_row(j), 0, 0)),
            const((1, d)), const(w_in.shape), const((1, d)), const((1, d)),
            const(w_s.shape), const(bias.shape), const(w_out.shape), const((1, d)),
        ],
        out_specs=pl.BlockSpec((1, tm, d), lambda j, i: (j, i, 0)),
        out_shape=jax.ShapeDtypeStruct((b, seq, d), F32),
        scratch_shapes=[pltpu.VMEM((tm, d), BF16)],
        compiler_params=pltpu.CompilerParams(
            dimension_semantics=("arbitrary", "arbitrary"), vmem_limit_bytes=VMEM_LIMIT),
        name="gmlp_layer",
    )(x, mod, npre, w_in, lng, lnb, w_s, bias, w_out, npost)


def _swap16(w):
    half = QK_ROPE // 4
    parts = [w[..., k * half:(k + 1) * half] for k in range(4)]
    return jnp.concatenate([parts[1], parts[0], parts[3], parts[2]], axis=-1)


def _rope_tables(seq):
    t = jnp.arange(seq)
    half = QK_ROPE // 4
    inv = ROPE_THETA ** (-jnp.arange(half, dtype=F32) / half)
    ang_r = (t // GRID_W).astype(F32)[:, None] * inv
    ang_c = (t % GRID_W).astype(F32)[:, None] * inv
    cr, sr, cc, sc = jnp.cos(ang_r), jnp.sin(ang_r), jnp.cos(ang_c), jnp.sin(ang_c)
    zero = jnp.zeros((seq, QK_ROPE), F32)
    cos = jnp.concatenate([cr, cr, cc, cc, zero], axis=-1)
    sin = jnp.concatenate([-sr, sr, -sc, sc, zero], axis=-1)
    return cos, sin


def kernel(x_prompt, x_sample, cache_ckv, cache_kpe, c, c_ctx, w_ada, b_ada, norm_pre, norm_post,
           w_in_ap, q_norm, w_uq, kv_norm, w_ukv, w_pool, pool_scale, w_out_ap,
           w_in_c, sgu_ln_g, sgu_ln_b, w_s, b_s, w_out_c):
    d = D_MODEL
    dec_b = x_sample.shape[0]
    ctx_row = dec_b

    cond = jnp.zeros((MOD_ROWS, d), F32).at[:dec_b].set(c).at[ctx_row].set(c_ctx)
    mod = _ada(cond, w_ada, b_ada)
    mod0 = mod[0].reshape(MOD_ROWS, 1, 3 * d)
    mod1 = mod[1].reshape(MOD_ROWS, 1, 3 * d)
    row_sample = lambda j: j
    row_prompt = lambda j: ctx_row

    wi = w_in_ap[0]
    q_lat, kv_lat, k_pe, gate_a, pool_in, gate_b = jnp.split(
        wi, (Q_LORA, Q_LORA + KV_LORA, Q_LORA + KV_LORA + QK_ROPE,
             Q_LORA + KV_LORA + QK_ROPE + MLA_WIDTH,
             Q_LORA + KV_LORA + QK_ROPE + MLA_WIDTH + POOL_WIDTH), axis=1)
    body = [q_lat, kv_lat, gate_a, pool_in, gate_b, k_pe]
    w_in_rope = jnp.concatenate(body + [_swap16(k_pe)], axis=1).astype(BF16)
    w_in_plain = jnp.concatenate(body + [jnp.zeros_like(k_pe)], axis=1).astype(BF16)

    wq = w_uq[0].reshape(Q_LORA, MLA_HEADS, QK_NOPE + QK_ROPE)
    wq_n, wq_p = wq[..., :QK_NOPE], wq[..., QK_NOPE:]
    w_uq_rope = jnp.concatenate([wq_n, wq_p, _swap16(wq_p)], axis=-1).reshape(Q_LORA, -1).astype(BF16)
    w_uq_plain = jnp.concatenate([wq_n, wq_p, jnp.zeros_like(wq_p)], axis=-1).reshape(Q_LORA, -1).astype(BF16)

    wkv = w_ukv[0].reshape(KV_LORA, MLA_HEADS, QK_NOPE + V_DIM)
    w_ukt = wkv[..., :QK_NOPE].transpose(1, 2, 0).astype(BF16)
    w_uvt = wkv[..., QK_NOPE:].transpose(1, 2, 0).astype(BF16)

    w_pool_b = w_pool[0].astype(BF16)
    w_out_b = w_out_ap[0].astype(BF16)
    npre0, npost0 = norm_pre[0][None], norm_post[0][None]
    npre1, npost1 = norm_pre[1][None], norm_post[1][None]
    qn, kvn, pscale = q_norm[0][None], kv_norm[0][None], pool_scale[0][None]

    w_in_c_b = w_in_c[0].astype(BF16)
    w_s_b = w_s[0].astype(BF16)
    bias = jnp.repeat(b_s[0].T, SGU_GROUP_DIM, axis=1)
    w_out_c_b = w_out_c[0].astype(BF16)
    lng, lnb = sgu_ln_g[0][None], sgu_ln_b[0][None]

    seq_p = x_prompt.shape[1]
    qp, kp, ltp, gap, ppp, ckv_new, kpe_new = _front(
        x_prompt, mod0, row_prompt, npre0, w_in_plain, qn, w_uq_plain, w_ukt, kvn,
        w_pool_b, pscale, None, tm=seq_p, emit_cache=True)
    xp1 = _attend(qp, kp, ltp, None, w_uvt, gap, ppp, x_prompt, mod0, row_prompt, npost0, w_out_b,
                  tq=seq_p)
    y_prompt = _gmlp(xp1, mod1, row_prompt, npre1, w_in_c_b, lng, lnb, w_s_b, bias, w_out_c_b, npost1,
                     tm=seq_p)

    seq_s = x_sample.shape[1]
    kpe_pad = jnp.pad(cache_kpe[:, 0], ((0, 0), (0, 0), (0, QK_ROPE)))
    ctx = _ctx_keys(cache_ckv[:, 0], kpe_pad)
    qs, ks, lts, gas, pps = _front(
        x_sample, mod0, row_sample, npre0, w_in_rope, qn, w_uq_rope, w_ukt, kvn,
        w_pool_b, pscale, _rope_tables(seq_s), tm=512, emit_cache=False)
    xs1 = _attend(qs, ks, lts, ctx, w_uvt, gas, pps, x_sample, mod0, row_sample, npost0, w_out_b,
                  tq=512)
    y_sample = _gmlp(xs1, mod1, row_sample, npre1, w_in_c_b, lng, lnb, w_s_b, bias, w_out_c_b, npost1,
                     tm=512)

    return (y_prompt, y_sample, ckv_new[:, None], kpe_new[:, None])
```

```python
import functools

import jax
import jax.numpy as jnp
import numpy as np
from jax import lax
from jax.experimental import pallas as pl
from jax.experimental.pallas import tpu as pltpu

D_MODEL = 1024
EPS = 1e-6
MLA_HEADS = 4
Q_LORA = 256
KV_LORA = 128
QK_NOPE = 128
QK_ROPE = 64
V_DIM = 128
ROPE_THETA = 10000.0
GRID_W = 64
POOL_WINDOWS = (2, 4, 8, 16)
POOL_GROUP = 128
POOL_WIDTH = len(POOL_WINDOWS) * POOL_GROUP
MLA_WIDTH = MLA_HEADS * V_DIM
CHUNK = 128
SGU_GROUPS = 4
SGU_GROUP_DIM = D_MODEL // SGU_GROUPS
LOG2_E = 1.4426950408889634
Q_SCALE = (QK_NOPE + QK_ROPE) ** -0.5 * LOG2_E

QK_PAD = KV_LORA + 2 * QK_ROPE
BF16_SUBLANES = 16
LAT_ROWS = KV_LORA + BF16_SUBLANES
KEY_BLOCK = 256
POOL_HALO = 8
MOD_ROWS = 16
VMEM_LIMIT = 56 * 1024 * 1024

_C_QLAT = 0
_C_KVLAT = _C_QLAT + Q_LORA
_C_GATE_A = _C_KVLAT + KV_LORA
_C_POOL = _C_GATE_A + MLA_WIDTH
_C_GATE_B = _C_POOL + POOL_WIDTH
_C_KPE = _C_GATE_B + POOL_WIDTH
_C_END = _C_KPE + 2 * QK_ROPE

BF16 = jnp.bfloat16
F32 = jnp.float32


def _dot(a, b):
    return jnp.dot(a, b, preferred_element_type=F32)


def _dot_nt(a, b):
    return lax.dot_general(a, b, (((1,), (1,)), ((), ())), preferred_element_type=F32)


def _rms(x, g):
    return x * lax.rsqrt(jnp.mean(x * x, axis=-1, keepdims=True) + EPS) * g


def _ada_kernel(c_ref, w_ref, b_ref, o_ref):
    a = jax.nn.silu(c_ref[...]).astype(BF16)
    o_ref[0] = _dot(a, w_ref[0].astype(BF16)) + b_ref[0]


def _ada(cond, w_ada, b_ada):
    depth, d, n = w_ada.shape
    bn = 512
    return pl.pallas_call(
        _ada_kernel,
        grid=(depth, n // bn),
        in_specs=[
            pl.BlockSpec((MOD_ROWS, d), lambda l, j: (0, 0)),
            pl.BlockSpec((1, d, bn), lambda l, j: (l, 0, j)),
            pl.BlockSpec((1, 1, bn), lambda l, j: (l, 0, j)),
        ],
        out_specs=pl.BlockSpec((1, MOD_ROWS, bn), lambda l, j: (l, 0, j)),
        out_shape=jax.ShapeDtypeStruct((depth, MOD_ROWS, n), F32),
        name="ada_mod",
    )(cond, w_ada, b_ada.reshape(depth, 1, n))


def _front_kernel(*refs, tm, seq, halo, emit_cache):
    it = iter(refs)
    x_ref = next(it)
    xp_ref = next(it) if halo else None
    xn_ref = next(it) if halo else None
    mod_ref = next(it)
    npre_ref = next(it)
    win_ref = next(it)
    qn_ref = next(it)
    wq_ref = next(it)
    kvn_ref = next(it)
    wpool_ref = next(it)
    pscale_ref = next(it)
    cos_ref = next(it)
    sin_ref = next(it)
    q_ref = next(it)
    k_ref = next(it)
    lt_ref = next(it)
    ga_ref = next(it)
    pp_ref = next(it)
    ckv_ref = next(it) if emit_cache else None
    kpe_ref = next(it) if emit_cache else None

    i = pl.program_id(0)
    nt = pl.num_programs(0)
    d = D_MODEL
    shift = mod_ref[0, :, 0:d]
    scale = mod_ref[0, :, d:2 * d]
    npre = npre_ref[...]

    def modulate(xv):
        return (_rms(xv, npre) * (1.0 + scale) + shift).astype(BF16)

    def rotate(blk):
        return blk * cos_ref[...] + pltpu.roll(blk, QK_ROPE, axis=1) * sin_ref[...]

    h = modulate(x_ref[0])

    q_lat = _dot(h, win_ref[:, _C_QLAT:_C_KVLAT])
    qn = _rms(q_lat, qn_ref[...]).astype(BF16)
    qa = _dot(qn, wq_ref[:, 0:MLA_HEADS * KV_LORA]) * Q_SCALE
    qp = _dot(qn, wq_ref[:, MLA_HEADS * KV_LORA:]) * Q_SCALE
    for hd in range(MLA_HEADS):
        q_ref[0, hd, :, 0:KV_LORA] = qa[:, hd * KV_LORA:(hd + 1) * KV_LORA].astype(BF16)
        q_ref[0, hd, :, KV_LORA:QK_PAD] = rotate(
            qp[:, hd * 2 * QK_ROPE:(hd + 1) * 2 * QK_ROPE]).astype(BF16)

    ckv = _rms(_dot(h, win_ref[:, _C_KVLAT:_C_GATE_A]), kvn_ref[...])
    kpe2 = _dot(h, win_ref[:, _C_KPE:_C_END])
    if emit_cache:
        ckv_ref[0] = ckv
        kpe_ref[0] = kpe2[:, 0:QK_ROPE]
    k_ref[0, :, 0:KV_LORA] = ckv.astype(BF16)
    k_ref[0, :, KV_LORA:QK_PAD] = rotate(kpe2).astype(BF16)
    lt_ref[0, 0, 0:KV_LORA] = ckv.T.astype(BF16)
    lt_ref[0, 0, KV_LORA:LAT_ROWS] = jnp.ones((BF16_SUBLANES, tm), BF16)

    ga_ref[0] = jax.nn.silu(_dot(h, win_ref[:, _C_GATE_A:_C_POOL])).astype(BF16)

    u = _dot(h, win_ref[:, _C_POOL:_C_GATE_B])
    if halo:
        hh = modulate(jnp.concatenate([xp_ref[0], xn_ref[0]], axis=0))
        uh = _dot(hh, win_ref[:, _C_POOL:_C_GATE_B])
        up = jnp.where(i > 0, uh[0:POOL_HALO], 0.0)
        un = jnp.where(i < nt - 1, uh[POOL_HALO:2 * POOL_HALO], 0.0)
    else:
        up = jnp.zeros((POOL_HALO, POOL_WIDTH), F32)
        un = up
    ue = jnp.concatenate([up, u, un], axis=0)
    ext = tm + 2 * POOL_HALO
    t = i * tm + lax.broadcasted_iota(jnp.int32, (tm, 1), 0)
    gate_b = _dot(h, win_ref[:, _C_GATE_B:_C_KPE])
    for g, w in enumerate(POOL_WINDOWS):
        sl = slice(g * POOL_GROUP, (g + 1) * POOL_GROUP)
        p = ue[:, sl]
        k = 1
        while k < w:
            p = p + pltpu.roll(p, k, axis=0)
            k *= 2
        lead = w // 2 - 1
        if lead:
            p = pltpu.roll(p, ext - lead, axis=0)
        wsum = p[POOL_HALO:POOL_HALO + tm]
        cnt = (jnp.minimum(t + w // 2, seq) - jnp.maximum(t - w // 2, 0)).astype(F32)
        dlt = (wsum * (1.0 / cnt) - u[:, sl]).astype(BF16)
        og = _dot(dlt, wpool_ref[g]) * pscale_ref[:, sl]
        pp_ref[0, :, sl] = (og * jax.nn.silu(gate_b[:, sl])).astype(BF16)


def _front(x, mod, mod_row, npre, w_in, qn, w_q, kvn, w_pool, pscale, tables, *, tm, emit_cache):
    b, seq, d = x.shape
    nt = seq // tm
    halo = nt > 1
    hb = tm // POOL_HALO
    last = seq // POOL_HALO - 1

    def const(shape):
        return pl.BlockSpec(shape, lambda i, j: (0,) * len(shape))

    in_specs = [pl.BlockSpec((1, tm, d), lambda i, j: (j, i, 0))]
    args = [x]
    if halo:
        in_specs += [
            pl.BlockSpec((1, POOL_HALO, d), lambda i, j: (j, jnp.maximum(i * hb - 1, 0), 0)),
            pl.BlockSpec((1, POOL_HALO, d), lambda i, j: (j, jnp.minimum((i + 1) * hb, last), 0)),
        ]
        args += [x, x]
    in_specs += [
        pl.BlockSpec((1, 1, 3 * d), lambda i, j: (mod_row(j), 0, 0)),
        const((1, d)), const(w_in.shape), const((1, Q_LORA)), const(w_q.shape),
        const((1, KV_LORA)), const(w_pool.shape), const((1, POOL_WIDTH)),
        pl.BlockSpec((tm, 2 * QK_ROPE), lambda i, j: (i, 0)),
        pl.BlockSpec((tm, 2 * QK_ROPE), lambda i, j: (i, 0)),
    ]
    args += [mod, npre, w_in, qn, w_q, kvn, w_pool, pscale, *tables]

    out_specs = [
        pl.BlockSpec((1, MLA_HEADS, tm, QK_PAD), lambda i, j: (j, 0, i, 0)),
        pl.BlockSpec((1, tm, QK_PAD), lambda i, j: (j, i, 0)),
        pl.BlockSpec((1, 1, LAT_ROWS, tm), lambda i, j: (j, i, 0, 0)),
        pl.BlockSpec((1, tm, MLA_WIDTH), lambda i, j: (j, i, 0)),
        pl.BlockSpec((1, tm, POOL_WIDTH), lambda i, j: (j, i, 0)),
    ]
    out_shape = [
        jax.ShapeDtypeStruct((b, MLA_HEADS, seq, QK_PAD), BF16),
        jax.ShapeDtypeStruct((b, seq, QK_PAD), BF16),
        jax.ShapeDtypeStruct((b, nt, LAT_ROWS, tm), BF16),
        jax.ShapeDtypeStruct((b, seq, MLA_WIDTH), BF16),
        jax.ShapeDtypeStruct((b, seq, POOL_WIDTH), BF16),
    ]
    if emit_cache:
        out_specs += [
            pl.BlockSpec((1, tm, KV_LORA), lambda i, j: (j, i, 0)),
            pl.BlockSpec((1, tm, QK_ROPE), lambda i, j: (j, i, 0)),
        ]
        out_shape += [
            jax.ShapeDtypeStruct((b, seq, KV_LORA), F32),
            jax.ShapeDtypeStruct((b, seq, QK_ROPE), F32),
        ]
    return pl.pallas_call(
        functools.partial(_front_kernel, tm=tm, seq=seq, halo=halo, emit_cache=emit_cache),
        grid=(nt, b),
        in_specs=in_specs,
        out_specs=out_specs,
        out_shape=out_shape,
        compiler_params=pltpu.CompilerParams(
            dimension_semantics=("arbitrary", "arbitrary"), vmem_limit_bytes=VMEM_LIMIT),
        name="mla_pool_front",
    )(*args)


def _ctx_kernel(ckv_ref, kpe_ref, k_ref, lt_ref):
    ckv = ckv_ref[0]
    k_ref[0, :, 0:KV_LORA] = ckv.astype(BF16)
    k_ref[0, :, KV_LORA:QK_PAD] = kpe_ref[0].astype(BF16)
    lt_ref[0, 0:KV_LORA] = ckv.T.astype(BF16)
    lt_ref[0, KV_LORA:LAT_ROWS] = jnp.ones((BF16_SUBLANES, ckv.shape[0]), BF16)


def _ctx_keys(ckv, kpe_pad):
    b, past, _ = ckv.shape
    return pl.pallas_call(
        _ctx_kernel,
        grid=(b,),
        in_specs=[
            pl.BlockSpec((1, past, KV_LORA), lambda j: (j, 0, 0)),
            pl.BlockSpec((1, past, 2 * QK_ROPE), lambda j: (j, 0, 0)),
        ],
        out_specs=[
            pl.BlockSpec((1, past, QK_PAD), lambda j: (j, 0, 0)),
            pl.BlockSpec((1, LAT_ROWS, past), lambda j: (j, 0, 0)),
        ],
        out_shape=[
            jax.ShapeDtypeStruct((b, past, QK_PAD), BF16),
            jax.ShapeDtypeStruct((b, LAT_ROWS, past), BF16),
        ],
        name="ctx_keys",
    )(ckv, kpe_pad)


def _attn_kernel(*refs, tq, kc, nchunks, has_ctx):
    it = iter(refs)
    q_ref = next(it)
    k_ref = next(it)
    lt_ref = next(it)
    kc_ref = next(it) if has_ctx else None
    ltc_ref = next(it) if has_ctx else None
    wuvt_ref = next(it)
    ga_ref = next(it)
    pp_ref = next(it)
    x_ref = next(it)
    mod_ref = next(it)
    npost_ref = next(it)
    wout_ref = next(it)
    o_ref = next(it)
    sa_scr = next(it)
    sb_scr = next(it)
    mix_scr = next(it)

    d = D_MODEL
    sub = 8

    blocks = []
    for c in range(nchunks):
        for r in range(0, kc, KEY_BLOCK):
            blocks.append((c * kc + r, KEY_BLOCK,
                           k_ref.at[0, c * kc + r:c * kc + r + KEY_BLOCK, :],
                           lt_ref.at[0, c, :, r:r + KEY_BLOCK]))
    if has_ctx:
        blocks.append((nchunks * kc, kc_ref.shape[1], kc_ref.at[0], ltc_ref.at[0]))
    s_bufs = (sa_scr, sb_scr)

    def score_block(hd, bi, m8):
        r0, n, kblk, _ = blocks[bi]
        s = _dot_nt(kblk[...], q_ref[0, hd])
        s_bufs[hd % 2][r0:r0 + n, :] = s
        cm = jnp.max(s.reshape(n // sub, sub, tq), axis=0)
        return cm if m8 is None else jnp.maximum(m8, cm)

    m8 = None
    for bi in range(len(blocks)):
        m8 = score_block(0, bi, m8)
    for hd in range(MLA_HEADS):
        m = jnp.max(m8, axis=0, keepdims=True)
        m8 = None
        if hd + 1 < MLA_HEADS:
            for bi in range(len(blocks)):
                m8 = score_block(hd + 1, bi, m8)
        else:
            out = _dot(pp_ref[0], wout_ref[MLA_WIDTH:, :])
        acc = None
        for r0, n, _, ltblk in blocks:
            p = jnp.exp2(s_bufs[hd % 2][r0:r0 + n, :] - m).astype(BF16)
            part = _dot(ltblk[...], p)
            acc = part if acc is None else acc + part
        o_lat = (acc[0:KV_LORA] * (1.0 / acc[KV_LORA:KV_LORA + 1])).astype(BF16)
        o = _dot(wuvt_ref[hd], o_lat).T
        sl = slice(hd * V_DIM, (hd + 1) * V_DIM)
        mix_scr[:, sl] = (o * ga_ref[0, :, sl].astype(F32)).astype(BF16)

    out = out + _dot(mix_scr[...], wout_ref[0:MLA_WIDTH, :])
    gate = mod_ref[0, :, 2 * d:3 * d]
    o_ref[0] = x_ref[0] + gate * _rms(out, npost_ref[...])


def _attend(q, k, lt, ctx, w_uvt, ga, pp, x, mod, mod_row, npost, w_out, *, tq):
    b, seq, d = x.shape
    nchunks, kc = lt.shape[1], lt.shape[3]
    has_ctx = ctx is not None
    n_keys = seq + (ctx[0].shape[1] if has_ctx else 0)

    def const(shape):
        return pl.BlockSpec(shape, lambda j, i: (0,) * len(shape))

    in_specs = [
        pl.BlockSpec((1, MLA_HEADS, tq, QK_PAD), lambda j, i: (j, 0, i, 0)),
        pl.BlockSpec((1, seq, QK_PAD), lambda j, i: (j, 0, 0)),
        pl.BlockSpec((1, nchunks, LAT_ROWS, kc), lambda j, i: (j, 0, 0, 0)),
    ]
    args = [q, k, lt]
    if has_ctx:
        kctx, ltctx = ctx
        past = kctx.shape[1]
        in_specs += [
            pl.BlockSpec((1, past, QK_PAD), lambda j, i: (j, 0, 0)),
            pl.BlockSpec((1, LAT_ROWS, past), lambda j, i: (j, 0, 0)),
        ]
        args += [kctx, ltctx]
    in_specs += [
        const(w_uvt.shape),
        pl.BlockSpec((1, tq, MLA_WIDTH), lambda j, i: (j, i, 0)),
        pl.BlockSpec((1, tq, POOL_WIDTH), lambda j, i: (j, i, 0)),
        pl.BlockSpec((1, tq, d), lambda j, i: (j, i, 0)),
        pl.BlockSpec((1, 1, 3 * d), lambda j, i: (mod_row(j), 0, 0)),
        const((1, d)), const(w_out.shape),
    ]
    args += [w_uvt, ga, pp, x, mod, npost, w_out]
    return pl.pallas_call(
        functools.partial(_attn_kernel, tq=tq, kc=kc, nchunks=nchunks, has_ctx=has_ctx),
        grid=(b, seq // tq),
        in_specs=in_specs,
        out_specs=pl.BlockSpec((1, tq, d), lambda j, i: (j, i, 0)),
        out_shape=jax.ShapeDtypeStruct((b, seq, d), F32),
        scratch_shapes=[
            pltpu.VMEM((n_keys, tq), F32),
            pltpu.VMEM((n_keys, tq), F32),
            pltpu.VMEM((tq, MLA_WIDTH), BF16),
        ],
        compiler_params=pltpu.CompilerParams(
            dimension_semantics=("arbitrary", "arbitrary"), vmem_limit_bytes=VMEM_LIMIT),
        name="mla_attend_out",
    )(*args)


def _gmlp_kernel(x_ref, mod_ref, npre_ref, win_ref, lng_ref, lnb_ref, ws_ref, bs_ref, wout_ref,
                 npost_ref, o_ref, z_scr, *, tm):
    d = D_MODEL
    shift = mod_ref[0, :, 0:d]
    scale = mod_ref[0, :, d:2 * d]
    gate = mod_ref[0, :, 2 * d:3 * d]

    x = x_ref[0]
    h = (_rms(x, npre_ref[...]) * (1.0 + scale) + shift).astype(BF16)
    u = jax.nn.gelu(_dot(h, win_ref[:, 0:d]))
    v = jax.nn.gelu(_dot(h, win_ref[:, d:2 * d]))
    sg = jax.nn.silu(_dot(h, win_ref[:, 2 * d:3 * d]))
    mu = jnp.mean(v, axis=-1, keepdims=True)
    vc = v - mu
    var = jnp.mean(vc * vc, axis=-1, keepdims=True)
    vn = (vc * lax.rsqrt(var + EPS) * lng_ref[...] + lnb_ref[...]).astype(BF16)
    us = u * sg
    for n in range(tm // CHUNK):
        rows = slice(n * CHUNK, (n + 1) * CHUNK)
        for g in range(SGU_GROUPS):
            cols = slice(g * SGU_GROUP_DIM, (g + 1) * SGU_GROUP_DIM)
            sv = _dot(ws_ref[g], vn[rows, cols]) + bs_ref[:, cols]
            z_scr[rows, cols] = (us[rows, cols] * sv).astype(BF16)
    out = _dot(z_scr[...], wout_ref[...])
    o_ref[0] = x + gate * _rms(out, npost_ref[...])


def _gmlp(x, mod, mod_row, npre, w_in, lng, lnb, w_s, bias, w_out, npost, *, tm):
    b, seq, d = x.shape

    def const(shape):
        return pl.BlockSpec(shape, lambda j, i: (0,) * len(shape))

    return pl.pallas_call(
        functools.partial(_gmlp_kernel, tm=tm),
        grid=(b, seq // tm),
        in_specs=[
            pl.BlockSpec((1, tm, d), lambda j, i: (j, i, 0)),
            pl.BlockSpec((1, 1, 3 * d), lambda j, i: (mod_row(j), 0, 0)),
            const((1, d)), const(w_in.shape), const((1, d)), const((1, d)),
            const(w_s.shape), const(bias.shape), const(w_out.shape), const((1, d)),
        ],
        out_specs=pl.BlockSpec((1, tm, d), lambda j, i: (j, i, 0)),
        out_shape=jax.ShapeDtypeStruct((b, seq, d), F32),
        scratch_shapes=[pltpu.VMEM((tm, d), BF16)],
        compiler_params=pltpu.CompilerParams(
            dimension_semantics=("arbitrary", "arbitrary"), vmem_limit_bytes=VMEM_LIMIT),
        name="gmlp_layer",
    )(x, mod, npre, w_in, lng, lnb, w_s, bias, w_out, npost)


def _swap16(w):
    half = QK_ROPE // 4
    parts = [w[..., k * half:(k + 1) * half] for k in range(4)]
    return jnp.concatenate([parts[1], parts[0], parts[3], parts[2]], axis=-1)


def _rope_tables(seq):
    t = np.arange(seq)
    half = QK_ROPE // 4
    inv = ROPE_THETA ** (-np.arange(half, dtype=np.float64) / half)
    ang_r = (t // GRID_W)[:, None] * inv
    ang_c = (t % GRID_W)[:, None] * inv
    cr, sr, cc, sc = np.cos(ang_r), np.sin(ang_r), np.cos(ang_c), np.sin(ang_c)
    zero = np.zeros((seq, QK_ROPE))
    cos = np.concatenate([cr, cr, cc, cc, zero], axis=-1)
    sin = np.concatenate([-sr, sr, -sc, sc, zero], axis=-1)
    return jnp.asarray(cos, F32), jnp.asarray(sin, F32)


def _no_position_tables(seq):
    one = np.concatenate([np.ones((seq, QK_ROPE)), np.zeros((seq, QK_ROPE))], axis=-1)
    return jnp.asarray(one, F32), jnp.zeros((seq, 2 * QK_ROPE), F32)


def _absorb_kernel(wqn_ref, wukt_ref, o_ref):
    for hd in range(MLA_HEADS):
        o_ref[hd] = _dot(wqn_ref[hd].astype(BF16), wukt_ref[hd].astype(BF16)).astype(BF16)


def _absorb(wq_nope, w_ukt):
    return pl.pallas_call(
        _absorb_kernel,
        out_shape=jax.ShapeDtypeStruct((MLA_HEADS, Q_LORA, KV_LORA), BF16),
        name="absorb_q",
    )(wq_nope, w_ukt)


def kernel(x_prompt, x_sample, cache_ckv, cache_kpe, c, c_ctx, w_ada, b_ada, norm_pre, norm_post,
           w_in_ap, q_norm, w_uq, kv_norm, w_ukv, w_pool, pool_scale, w_out_ap,
           w_in_c, sgu_ln_g, sgu_ln_b, w_s, b_s, w_out_c):
    d = D_MODEL
    dec_b = x_sample.shape[0]
    ctx_row = dec_b

    cond = jnp.zeros((MOD_ROWS, d), F32).at[:dec_b].set(c).at[ctx_row].set(c_ctx)
    mod = _ada(cond, w_ada, b_ada)
    mod0 = mod[0].reshape(MOD_ROWS, 1, 3 * d)
    mod1 = mod[1].reshape(MOD_ROWS, 1, 3 * d)
    row_sample = lambda j: j
    row_prompt = lambda j: ctx_row

    wi = w_in_ap[0]
    q_lat, kv_lat, k_pe, gate_a, pool_in, gate_b = jnp.split(
        wi, (Q_LORA, Q_LORA + KV_LORA, Q_LORA + KV_LORA + QK_ROPE,
             Q_LORA + KV_LORA + QK_ROPE + MLA_WIDTH,
             Q_LORA + KV_LORA + QK_ROPE + MLA_WIDTH + POOL_WIDTH), axis=1)
    w_in = jnp.concatenate(
        [q_lat, kv_lat, gate_a, pool_in, gate_b, k_pe, _swap16(k_pe)], axis=1).astype(BF16)

    wkv = w_ukv[0].reshape(KV_LORA, MLA_HEADS, QK_NOPE + V_DIM)
    w_ukt = wkv[..., :QK_NOPE].transpose(1, 2, 0)
    w_uvt = wkv[..., QK_NOPE:].transpose(1, 2, 0).astype(BF16)

    wq = w_uq[0].reshape(Q_LORA, MLA_HEADS, QK_NOPE + QK_ROPE)
    wq_n, wq_p = wq[..., :QK_NOPE], wq[..., QK_NOPE:]
    w_qa = _absorb(wq_n.transpose(1, 0, 2), w_ukt)
    w_q = jnp.concatenate(
        [w_qa.transpose(1, 0, 2).reshape(Q_LORA, -1),
         jnp.concatenate([wq_p, _swap16(wq_p)], axis=-1).reshape(Q_LORA, -1).astype(BF16)], axis=1)

    w_pool_b = w_pool[0].astype(BF16)
    w_out_b = w_out_ap[0].astype(BF16)
    npre0, npost0 = norm_pre[0][None], norm_post[0][None]
    npre1, npost1 = norm_pre[1][None], norm_post[1][None]
    qn, kvn, pscale = q_norm[0][None], kv_norm[0][None], pool_scale[0][None]

    w_in_c_b = w_in_c[0].astype(BF16)
    w_s_b = w_s[0].astype(BF16)
    bias = jnp.repeat(b_s[0].T, SGU_GROUP_DIM, axis=1)
    w_out_c_b = w_out_c[0].astype(BF16)
    lng, lnb = sgu_ln_g[0][None], sgu_ln_b[0][None]

    seq_p = x_prompt.shape[1]
    qp, kp, ltp, gap, ppp, ckv_new, kpe_new = _front(
        x_prompt, mod0, row_prompt, npre0, w_in, qn, w_q, kvn,
        w_pool_b, pscale, _no_position_tables(seq_p), tm=seq_p, emit_cache=True)
    xp1 = _attend(qp, kp, ltp, None, w_uvt, gap, ppp, x_prompt, mod0, row_prompt, npost0, w_out_b,
                  tq=seq_p)
    y_prompt = _gmlp(xp1, mod1, row_prompt, npre1, w_in_c_b, lng, lnb, w_s_b, bias, w_out_c_b, npost1,
                     tm=seq_p)

    seq_s = x_sample.shape[1]
    kpe_pad = jnp.pad(cache_kpe[:, 0], ((0, 0), (0, 0), (0, QK_ROPE)))
    ctx = _ctx_keys(cache_ckv[:, 0], kpe_pad)
    qs, ks, lts, gas, pps = _front(
        x_sample, mod0, row_sample, npre0, w_in, qn, w_q, kvn,
        w_pool_b, pscale, _rope_tables(seq_s), tm=512, emit_cache=False)
    xs1 = _attend(qs, ks, lts, ctx, w_uvt, gas, pps, x_sample, mod0, row_sample, npost0, w_out_b,
                  tq=512)
    y_sample = _gmlp(xs1, mod1, row_sample, npre1, w_in_c_b, lng, lnb, w_s_b, bias, w_out_c_b, npost1,
                     tm=512)

    return (y_prompt, y_sample, ckv_new[:, None], kpe_new[:, None])
```

```python
import functools

import jax
import jax.numpy as jnp
import numpy as np
from jax import lax
from jax.experimental import pallas as pl
from jax.experimental.pallas import tpu as pltpu

D_MODEL = 1024
EPS = 1e-6
MLA_HEADS = 4
Q_LORA = 256
KV_LORA = 128
QK_NOPE = 128
QK_ROPE = 64
V_DIM = 128
ROPE_THETA = 10000.0
GRID_W = 64
POOL_WINDOWS = (2, 4, 8, 16)
POOL_GROUP = 128
POOL_WIDTH = len(POOL_WINDOWS) * POOL_GROUP
MLA_WIDTH = MLA_HEADS * V_DIM
CHUNK = 128
SGU_GROUPS = 4
SGU_GROUP_DIM = D_MODEL // SGU_GROUPS
LOG2_E = 1.4426950408889634
Q_SCALE = (QK_NOPE + QK_ROPE) ** -0.5 * LOG2_E

QK_PAD = KV_LORA + 2 * QK_ROPE
BF16_SUBLANES = 16
LAT_ROWS = KV_LORA + BF16_SUBLANES
KEY_BLOCK = 512
POOL_HALO = 8
MOD_ROWS = 16
VMEM_LIMIT = 56 * 1024 * 1024

_C_QLAT = 0
_C_KVLAT = _C_QLAT + Q_LORA
_C_GATE_A = _C_KVLAT + KV_LORA
_C_POOL = _C_GATE_A + MLA_WIDTH
_C_GATE_B = _C_POOL + POOL_WIDTH
_C_KPE = _C_GATE_B + POOL_WIDTH
_C_END = _C_KPE + 2 * QK_ROPE

BF16 = jnp.bfloat16
F32 = jnp.float32


def _dot(a, b):
    return jnp.dot(a, b, preferred_element_type=F32)


def _dot_nt(a, b):
    return lax.dot_general(a, b, (((1,), (1,)), ((), ())), preferred_element_type=F32)


def _rms(x, g):
    return x * lax.rsqrt(jnp.mean(x * x, axis=-1, keepdims=True) + EPS) * g


def _ada_kernel(c_ref, w_ref, b_ref, o_ref):
    a = jax.nn.silu(c_ref[...]).astype(BF16)
    o_ref[0] = _dot(a, w_ref[0].astype(BF16)) + b_ref[0]


def _ada(cond, w_ada, b_ada):
    depth, d, n = w_ada.shape
    bn = 512
    return pl.pallas_call(
        _ada_kernel,
        grid=(depth, n // bn),
        in_specs=[
            pl.BlockSpec((MOD_ROWS, d), lambda l, j: (0, 0)),
            pl.BlockSpec((1, d, bn), lambda l, j: (l, 0, j)),
            pl.BlockSpec((1, 1, bn), lambda l, j: (l, 0, j)),
        ],
        out_specs=pl.BlockSpec((1, MOD_ROWS, bn), lambda l, j: (l, 0, j)),
        out_shape=jax.ShapeDtypeStruct((depth, MOD_ROWS, n), F32),
        name="ada_mod",
    )(cond, w_ada, b_ada.reshape(depth, 1, n))


def _front_kernel(*refs, tm, seq, halo, emit_cache):
    it = iter(refs)
    x_ref = next(it)
    xp_ref = next(it) if halo else None
    xn_ref = next(it) if halo else None
    mod_ref = next(it)
    npre_ref = next(it)
    win_ref = next(it)
    qn_ref = next(it)
    wq_ref = next(it)
    kvn_ref = next(it)
    wpool_ref = next(it)
    pscale_ref = next(it)
    cos_ref = next(it)
    sin_ref = next(it)
    q_ref = next(it)
    k_ref = next(it)
    lt_ref = next(it)
    ga_ref = next(it)
    pp_ref = next(it)
    ckv_ref = next(it) if emit_cache else None
    kpe_ref = next(it) if emit_cache else None

    i = pl.program_id(0)
    nt = pl.num_programs(0)
    d = D_MODEL
    shift = mod_ref[0, :, 0:d]
    scale = mod_ref[0, :, d:2 * d]
    npre = npre_ref[...]

    def modulate(xv):
        return (_rms(xv, npre) * (1.0 + scale) + shift).astype(BF16)

    def rotate(blk):
        return blk * cos_ref[...] + pltpu.roll(blk, QK_ROPE, axis=1) * sin_ref[...]

    h = modulate(x_ref[0])

    q_lat = _dot(h, win_ref[:, _C_QLAT:_C_KVLAT])
    qn = _rms(q_lat, qn_ref[...]).astype(BF16)
    qa = _dot(qn, wq_ref[:, 0:MLA_HEADS * KV_LORA]) * Q_SCALE
    qp = _dot(qn, wq_ref[:, MLA_HEADS * KV_LORA:]) * Q_SCALE
    for hd in range(MLA_HEADS):
        q_ref[0, hd, :, 0:KV_LORA] = qa[:, hd * KV_LORA:(hd + 1) * KV_LORA].astype(BF16)
        q_ref[0, hd, :, KV_LORA:QK_PAD] = rotate(
            qp[:, hd * 2 * QK_ROPE:(hd + 1) * 2 * QK_ROPE]).astype(BF16)

    ckv = _rms(_dot(h, win_ref[:, _C_KVLAT:_C_GATE_A]), kvn_ref[...])
    kpe2 = _dot(h, win_ref[:, _C_KPE:_C_END])
    if emit_cache:
        ckv_ref[0] = ckv
        kpe_ref[0] = kpe2[:, 0:QK_ROPE]
    k_ref[0, :, 0:KV_LORA] = ckv.astype(BF16)
    k_ref[0, :, KV_LORA:QK_PAD] = rotate(kpe2).astype(BF16)
    lt_ref[0, 0, 0:KV_LORA] = ckv.T.astype(BF16)
    lt_ref[0, 0, KV_LORA:LAT_ROWS] = jnp.ones((BF16_SUBLANES, tm), BF16)

    ga_ref[0] = jax.nn.silu(_dot(h, win_ref[:, _C_GATE_A:_C_POOL])).astype(BF16)

    u = _dot(h, win_ref[:, _C_POOL:_C_GATE_B])
    if halo:
        hh = modulate(jnp.concatenate([xp_ref[0], xn_ref[0]], axis=0))
        uh = _dot(hh, win_ref[:, _C_POOL:_C_GATE_B])
        up = jnp.where(i > 0, uh[0:POOL_HALO], 0.0)
        un = jnp.where(i < nt - 1, uh[POOL_HALO:2 * POOL_HALO], 0.0)
    else:
        up = jnp.zeros((POOL_HALO, POOL_WIDTH), F32)
        un = up
    ue = jnp.concatenate([up, u, un], axis=0)
    ext = tm + 2 * POOL_HALO
    t = i * tm + lax.broadcasted_iota(jnp.int32, (tm, 1), 0)
    gate_b = _dot(h, win_ref[:, _C_GATE_B:_C_KPE])
    for g, w in enumerate(POOL_WINDOWS):
        sl = slice(g * POOL_GROUP, (g + 1) * POOL_GROUP)
        p = ue[:, sl]
        k = 1
        while k < w:
            p = p + pltpu.roll(p, k, axis=0)
            k *= 2
        lead = w // 2 - 1
        if lead:
            p = pltpu.roll(p, ext - lead, axis=0)
        wsum = p[POOL_HALO:POOL_HALO + tm]
        cnt = (jnp.minimum(t + w // 2, seq) - jnp.maximum(t - w // 2, 0)).astype(F32)
        dlt = (wsum * (1.0 / cnt) - u[:, sl]).astype(BF16)
        og = _dot(dlt, wpool_ref[g]) * pscale_ref[:, sl]
        pp_ref[0, :, sl] = (og * jax.nn.silu(gate_b[:, sl])).astype(BF16)


def _front(x, mod, mod_row, npre, w_in, qn, w_q, kvn, w_pool, pscale, tables, *, tm, emit_cache):
    b, seq, d = x.shape
    nt = seq // tm
    halo = nt > 1
    hb = tm // POOL_HALO
    last = seq // POOL_HALO - 1

    def const(shape):
        return pl.BlockSpec(shape, lambda i, j: (0,) * len(shape))

    in_specs = [pl.BlockSpec((1, tm, d), lambda i, j: (j, i, 0))]
    args = [x]
    if halo:
        in_specs += [
            pl.BlockSpec((1, POOL_HALO, d), lambda i, j: (j, jnp.maximum(i * hb - 1, 0), 0)),
            pl.BlockSpec((1, POOL_HALO, d), lambda i, j: (j, jnp.minimum((i + 1) * hb, last), 0)),
        ]
        args += [x, x]
    in_specs += [
        pl.BlockSpec((1, 1, 3 * d), lambda i, j: (mod_row(j), 0, 0)),
        const((1, d)), const(w_in.shape), const((1, Q_LORA)), const(w_q.shape),
        const((1, KV_LORA)), const(w_pool.shape), const((1, POOL_WIDTH)),
        pl.BlockSpec((tm, 2 * QK_ROPE), lambda i, j: (i, 0)),
        pl.BlockSpec((tm, 2 * QK_ROPE), lambda i, j: (i, 0)),
    ]
    args += [mod, npre, w_in, qn, w_q, kvn, w_pool, pscale, *tables]

    out_specs = [
        pl.BlockSpec((1, MLA_HEADS, tm, QK_PAD), lambda i, j: (j, 0, i, 0)),
        pl.BlockSpec((1, tm, QK_PAD), lambda i, j: (j, i, 0)),
        pl.BlockSpec((1, 1, LAT_ROWS, tm), lambda i, j: (j, i, 0, 0)),
        pl.BlockSpec((1, tm, MLA_WIDTH), lambda i, j: (j, i, 0)),
        pl.BlockSpec((1, tm, POOL_WIDTH), lambda i, j: (j, i, 0)),
    ]
    out_shape = [
        jax.ShapeDtypeStruct((b, MLA_HEADS, seq, QK_PAD), BF16),
        jax.ShapeDtypeStruct((b, seq, QK_PAD), BF16),
        jax.ShapeDtypeStruct((b, nt, LAT_ROWS, tm), BF16),
        jax.ShapeDtypeStruct((b, seq, MLA_WIDTH), BF16),
        jax.ShapeDtypeStruct((b, seq, POOL_WIDTH), BF16),
    ]
    if emit_cache:
        out_specs += [
            pl.BlockSpec((1, tm, KV_LORA), lambda i, j: (j, i, 0)),
            pl.BlockSpec((1, tm, QK_ROPE), lambda i, j: (j, i, 0)),
        ]
        out_shape += [
            jax.ShapeDtypeStruct((b, seq, KV_LORA), F32),
            jax.ShapeDtypeStruct((b, seq, QK_ROPE), F32),
        ]
    return pl.pallas_call(
        functools.partial(_front_kernel, tm=tm, seq=seq, halo=halo, emit_cache=emit_cache),
        grid=(nt, b),
        in_specs=in_specs,
        out_specs=out_specs,
        out_shape=out_shape,
        compiler_params=pltpu.CompilerParams(
            dimension_semantics=("arbitrary", "arbitrary"), vmem_limit_bytes=VMEM_LIMIT),
        name="mla_pool_front",
    )(*args)


def _ctx_kernel(ckv_ref, kpe_ref, k_ref, lt_ref):
    ckv = ckv_ref[0]
    k_ref[0, :, 0:KV_LORA] = ckv.astype(BF16)
    k_ref[0, :, KV_LORA:QK_PAD] = kpe_ref[0].astype(BF16)
    lt_ref[0, 0:KV_LORA] = ckv.T.astype(BF16)
    lt_ref[0, KV_LORA:LAT_ROWS] = jnp.ones((BF16_SUBLANES, ckv.shape[0]), BF16)


def _ctx_keys(ckv, kpe_pad):
    b, past, _ = ckv.shape
    return pl.pallas_call(
        _ctx_kernel,
        grid=(b,),
        in_specs=[
            pl.BlockSpec((1, past, KV_LORA), lambda j: (j, 0, 0)),
            pl.BlockSpec((1, past, 2 * QK_ROPE), lambda j: (j, 0, 0)),
        ],
        out_specs=[
            pl.BlockSpec((1, past, QK_PAD), lambda j: (j, 0, 0)),
            pl.BlockSpec((1, LAT_ROWS, past), lambda j: (j, 0, 0)),
        ],
        out_shape=[
            jax.ShapeDtypeStruct((b, past, QK_PAD), BF16),
            jax.ShapeDtypeStruct((b, LAT_ROWS, past), BF16),
        ],
        name="ctx_keys",
    )(ckv, kpe_pad)


def _attn_kernel(*refs, tq, kc, nchunks, has_ctx):
    it = iter(refs)
    q_ref = next(it)
    k_ref = next(it)
    lt_ref = next(it)
    kc_ref = next(it) if has_ctx else None
    ltc_ref = next(it) if has_ctx else None
    wuvt_ref = next(it)
    ga_ref = next(it)
    pp_ref = next(it)
    x_ref = next(it)
    mod_ref = next(it)
    npost_ref = next(it)
    wout_ref = next(it)
    o_ref = next(it)
    mix_scr = next(it)

    d = D_MODEL
    sub = 8

    blocks = []
    kb = min(KEY_BLOCK, kc)
    for c in range(nchunks):
        for r in range(0, kc, kb):
            blocks.append((c * kc + r, kb,
                           k_ref.at[0, c * kc + r:c * kc + r + kb, :],
                           lt_ref.at[0, c, :, r:r + kb]))
    if has_ctx:
        blocks.append((nchunks * kc, kc_ref.shape[1], kc_ref.at[0], ltc_ref.at[0]))

    def score_head(hd):
        m8 = None
        scores = []
        for _, n, kblk, _ in blocks:
            s = _dot_nt(kblk[...], q_ref[0, hd])
            scores.append(s)
            cm = jnp.max(s.reshape(n // sub, sub, tq), axis=0)
            m8 = cm if m8 is None else jnp.maximum(m8, cm)
        return scores, m8

    scores, m8 = score_head(0)
    for hd in range(MLA_HEADS):
        m = jnp.max(m8, axis=0, keepdims=True)
        current = scores
        if hd + 1 < MLA_HEADS:
            scores, m8 = score_head(hd + 1)
        else:
            out = _dot(pp_ref[0], wout_ref[MLA_WIDTH:, :])
        acc = None
        for (_, _, _, ltblk), s in zip(blocks, current):
            p = jnp.exp2(s - m).astype(BF16)
            part = _dot(ltblk[...], p)
            acc = part if acc is None else acc + part
        o_lat = (acc[0:KV_LORA] * (1.0 / acc[KV_LORA:KV_LORA + 1])).astype(BF16)
        o = _dot(wuvt_ref[hd], o_lat).T
        sl = slice(hd * V_DIM, (hd + 1) * V_DIM)
        mix_scr[:, sl] = (o * ga_ref[0, :, sl].astype(F32)).astype(BF16)

    out = out + _dot(mix_scr[...], wout_ref[0:MLA_WIDTH, :])
    gate = mod_ref[0, :, 2 * d:3 * d]
    o_ref[0] = x_ref[0] + gate * _rms(out, npost_ref[...])


def _attend(q, k, lt, ctx, w_uvt, ga, pp, x, mod, mod_row, npost, w_out, *, tq):
    b, seq, d = x.shape
    nchunks, kc = lt.shape[1], lt.shape[3]
    has_ctx = ctx is not None

    def const(shape):
        return pl.BlockSpec(shape, lambda j, i: (0,) * len(shape))

    in_specs = [
        pl.BlockSpec((1, MLA_HEADS, tq, QK_PAD), lambda j, i: (j, 0, i, 0)),
        pl.BlockSpec((1, seq, QK_PAD), lambda j, i: (j, 0, 0)),
        pl.BlockSpec((1, nchunks, LAT_ROWS, kc), lambda j, i: (j, 0, 0, 0)),
    ]
    args = [q, k, lt]
    if has_ctx:
        kctx, ltctx = ctx
        past = kctx.shape[1]
        in_specs += [
            pl.BlockSpec((1, past, QK_PAD), lambda j, i: (j, 0, 0)),
            pl.BlockSpec((1, LAT_ROWS, past), lambda j, i: (j, 0, 0)),
        ]
        args += [kctx, ltctx]
    in_specs += [
        const(w_uvt.shape),
        pl.BlockSpec((1, tq, MLA_WIDTH), lambda j, i: (j, i, 0)),
        pl.BlockSpec((1, tq, POOL_WIDTH), lambda j, i: (j, i, 0)),
        pl.BlockSpec((1, tq, d), lambda j, i: (j, i, 0)),
        pl.BlockSpec((1, 1, 3 * d), lambda j, i: (mod_row(j), 0, 0)),
        const((1, d)), const(w_out.shape),
    ]
    args += [w_uvt, ga, pp, x, mod, npost, w_out]
    return pl.pallas_call(
        functools.partial(_attn_kernel, tq=tq, kc=kc, nchunks=nchunks, has_ctx=has_ctx),
        grid=(b, seq // tq),
        in_specs=in_specs,
        out_specs=pl.BlockSpec((1, tq, d), lambda j, i: (j, i, 0)),
        out_shape=jax.ShapeDtypeStruct((b, seq, d), F32),
        scratch_shapes=[pltpu.VMEM((tq, MLA_WIDTH), BF16)],
        compiler_params=pltpu.CompilerParams(
            dimension_semantics=("arbitrary", "arbitrary"), vmem_limit_bytes=VMEM_LIMIT),
        name="mla_attend_out",
    )(*args)


def _gmlp_kernel(x_ref, mod_ref, npre_ref, win_ref, lng_ref, lnb_ref, ws_ref, bs_ref, wout_ref,
                 npost_ref, o_ref, z_scr, *, tm):
    d = D_MODEL
    shift = mod_ref[0, :, 0:d]
    scale = mod_ref[0, :, d:2 * d]
    gate = mod_ref[0, :, 2 * d:3 * d]

    x = x_ref[0]
    h = (_rms(x, npre_ref[...]) * (1.0 + scale) + shift).astype(BF16)
    u = jax.nn.gelu(_dot(h, win_ref[:, 0:d]))
    v = jax.nn.gelu(_dot(h, win_ref[:, d:2 * d]))
    sg = jax.nn.silu(_dot(h, win_ref[:, 2 * d:3 * d]))
    mu = jnp.mean(v, axis=-1, keepdims=True)
    vc = v - mu
    var = jnp.mean(vc * vc, axis=-1, keepdims=True)
    vn = (vc * lax.rsqrt(var + EPS) * lng_ref[...] + lnb_ref[...]).astype(BF16)
    us = u * sg
    for n in range(tm // CHUNK):
        rows = slice(n * CHUNK, (n + 1) * CHUNK)
        for g in range(SGU_GROUPS):
            cols = slice(g * SGU_GROUP_DIM, (g + 1) * SGU_GROUP_DIM)
            sv = _dot(ws_ref[g], vn[rows, cols]) + bs_ref[:, cols]
            z_scr[rows, cols] = (us[rows, cols] * sv).astype(BF16)
    out = _dot(z_scr[...], wout_ref[...])
    o_ref[0] = x + gate * _rms(out, npost_ref[...])


def _gmlp(x, mod, mod_row, npre, w_in, lng, lnb, w_s, bias, w_out, npost, *, tm):
    b, seq, d = x.shape

    def const(shape):
        return pl.BlockSpec(shape, lambda j, i: (0,) * len(shape))

    return pl.pallas_call(
        functools.partial(_gmlp_kernel, tm=tm),
        grid=(b, seq // tm),
        in_specs=[
            pl.BlockSpec((1, tm, d), lambda j, i: (j, i, 0)),
            pl.BlockSpec((1, 1, 3 * d), lambda j, i: (mod_row(j), 0, 0)),
            const((1, d)), const(w_in.shape), const((1, d)), const((1, d)),
            const(w_s.shape), const(bias.shape), const(w_out.shape), const((1, d)),
        ],
        out_specs=pl.BlockSpec((1, tm, d), lambda j, i: (j, i, 0)),
        out_shape=jax.ShapeDtypeStruct((b, seq, d), F32),
        scratch_shapes=[pltpu.VMEM((tm, d), BF16)],
        compiler_params=pltpu.CompilerParams(
            dimension_semantics=("arbitrary", "arbitrary"), vmem_limit_bytes=VMEM_LIMIT),
        name="gmlp_layer",
    )(x, mod, npre, w_in, lng, lnb, w_s, bias, w_out, npost)


def _swap16(w):
    half = QK_ROPE // 4
    parts = [w[..., k * half:(k + 1) * half] for k in range(4)]
    return jnp.concatenate([parts[1], parts[0], parts[3], parts[2]], axis=-1)


def _rope_tables(seq):
    t = np.arange(seq)
    half = QK_ROPE // 4
    inv = ROPE_THETA ** (-np.arange(half, dtype=np.float64) / half)
    ang_r = (t // GRID_W)[:, None] * inv
    ang_c = (t % GRID_W)[:, None] * inv
    cr, sr, cc, sc = np.cos(ang_r), np.sin(ang_r), np.cos(ang_c), np.sin(ang_c)
    zero = np.zeros((seq, QK_ROPE))
    cos = np.concatenate([cr, cr, cc, cc, zero], axis=-1)
    sin = np.concatenate([-sr, sr, -sc, sc, zero], axis=-1)
    return jnp.asarray(cos, F32), jnp.asarray(sin, F32)


def _no_position_tables(seq):
    one = np.concatenate([np.ones((seq, QK_ROPE)), np.zeros((seq, QK_ROPE))], axis=-1)
    return jnp.asarray(one, F32), jnp.zeros((seq, 2 * QK_ROPE), F32)


def _absorb_kernel(wqn_ref, wukt_ref, o_ref):
    for hd in range(MLA_HEADS):
        o_ref[hd] = _dot(wqn_ref[hd].astype(BF16), wukt_ref[hd].astype(BF16)).astype(BF16)


def _absorb(wq_nope, w_ukt):
    return pl.pallas_call(
        _absorb_kernel,
        out_shape=jax.ShapeDtypeStruct((MLA_HEADS, Q_LORA, KV_LORA), BF16),
        name="absorb_q",
    )(wq_nope, w_ukt)


def kernel(x_prompt, x_sample, cache_ckv, cache_kpe, c, c_ctx, w_ada, b_ada, norm_pre, norm_post,
           w_in_ap, q_norm, w_uq, kv_norm, w_ukv, w_pool, pool_scale, w_out_ap,
           w_in_c, sgu_ln_g, sgu_ln_b, w_s, b_s, w_out_c):
    d = D_MODEL
    dec_b = x_sample.shape[0]
    ctx_row = dec_b

    cond = jnp.zeros((MOD_ROWS, d), F32).at[:dec_b].set(c).at[ctx_row].set(c_ctx)
    mod = _ada(cond, w_ada, b_ada)
    mod0 = mod[0].reshape(MOD_ROWS, 1, 3 * d)
    mod1 = mod[1].reshape(MOD_ROWS, 1, 3 * d)
    row_sample = lambda j: j
    row_prompt = lambda j: ctx_row

    wi = w_in_ap[0]
    q_lat, kv_lat, k_pe, gate_a, pool_in, gate_b = jnp.split(
        wi, (Q_LORA, Q_LORA + KV_LORA, Q_LORA + KV_LORA + QK_ROPE,
             Q_LORA + KV_LORA + QK_ROPE + MLA_WIDTH,
             Q_LORA + KV_LORA + QK_ROPE + MLA_WIDTH + POOL_WIDTH), axis=1)
    w_in = jnp.concatenate(
        [q_lat, kv_lat, gate_a, pool_in, gate_b, k_pe, _swap16(k_pe)], axis=1).astype(BF16)

    wkv = w_ukv[0].reshape(KV_LORA, MLA_HEADS, QK_NOPE + V_DIM)
    w_ukt = wkv[..., :QK_NOPE].transpose(1, 2, 0)
    w_uvt = wkv[..., QK_NOPE:].transpose(1, 2, 0).astype(BF16)

    wq = w_uq[0].reshape(Q_LORA, MLA_HEADS, QK_NOPE + QK_ROPE)
    wq_n, wq_p = wq[..., :QK_NOPE], wq[..., QK_NOPE:]
    w_qa = _absorb(wq_n.transpose(1, 0, 2), w_ukt)
    w_q = jnp.concatenate(
        [w_qa.transpose(1, 0, 2).reshape(Q_LORA, -1),
         jnp.concatenate([wq_p, _swap16(wq_p)], axis=-1).reshape(Q_LORA, -1).astype(BF16)], axis=1)

    w_pool_b = w_pool[0].astype(BF16)
    w_out_b = w_out_ap[0].astype(BF16)
    npre0, npost0 = norm_pre[0][None], norm_post[0][None]
    npre1, npost1 = norm_pre[1][None], norm_post[1][None]
    qn, kvn, pscale = q_norm[0][None], kv_norm[0][None], pool_scale[0][None]

    w_in_c_b = w_in_c[0].astype(BF16)
    w_s_b = w_s[0].astype(BF16)
    bias = jnp.repeat(b_s[0].T, SGU_GROUP_DIM, axis=1)
    w_out_c_b = w_out_c[0].astype(BF16)
    lng, lnb = sgu_ln_g[0][None], sgu_ln_b[0][None]

    seq_p = x_prompt.shape[1]
    qp, kp, ltp, gap, ppp, ckv_new, kpe_new = _front(
        x_prompt, mod0, row_prompt, npre0, w_in, qn, w_q, kvn,
        w_pool_b, pscale, _no_position_tables(seq_p), tm=seq_p, emit_cache=True)
    xp1 = _attend(qp, kp, ltp, None, w_uvt, gap, ppp, x_prompt, mod0, row_prompt, npost0, w_out_b,
                  tq=seq_p)
    pair = xp1.reshape(xp1.shape[0] // 2, 2 * seq_p, d)
    y_prompt = _gmlp(pair, mod1, row_prompt, npre1, w_in_c_b, lng, lnb, w_s_b, bias, w_out_c_b, npost1,
                     tm=2 * seq_p).reshape(xp1.shape)

    seq_s = x_sample.shape[1]
    kpe_pad = jnp.pad(cache_kpe[:, 0], ((0, 0), (0, 0), (0, QK_ROPE)))
    ctx = _ctx_keys(cache_ckv[:, 0], kpe_pad)
    qs, ks, lts, gas, pps = _front(
        x_sample, mod0, row_sample, npre0, w_in, qn, w_q, kvn,
        w_pool_b, pscale, _rope_tables(seq_s), tm=512, emit_cache=False)
    xs1 = _attend(qs, ks, lts, ctx, w_uvt, gas, pps, x_sample, mod0, row_sample, npost0, w_out_b,
                  tq=512)
    y_sample = _gmlp(xs1, mod1, row_sample, npre1, w_in_c_b, lng, lnb, w_s_b, bias, w_out_c_b, npost1,
                     tm=512)

    return (y_prompt, y_sample, ckv_new[:, None], kpe_new[:, None])
```

```python
import functools

import jax
import jax.numpy as jnp
import numpy as np
from jax import lax
from jax.experimental import pallas as pl
from jax.experimental.pallas import tpu as pltpu

D_MODEL = 1024
EPS = 1e-6
MLA_HEADS = 4
Q_LORA = 256
KV_LORA = 128
QK_NOPE = 128
QK_ROPE = 64
V_DIM = 128
ROPE_THETA = 10000.0
GRID_W = 64
POOL_WINDOWS = (2, 4, 8, 16)
POOL_GROUP = 128
POOL_WIDTH = len(POOL_WINDOWS) * POOL_GROUP
MLA_WIDTH = MLA_HEADS * V_DIM
CHUNK = 128
SGU_GROUPS = 4
SGU_GROUP_DIM = D_MODEL // SGU_GROUPS
LOG2_E = 1.4426950408889634
Q_SCALE = (QK_NOPE + QK_ROPE) ** -0.5 * LOG2_E

QK_PAD = KV_LORA + 2 * QK_ROPE
BF16_SUBLANES = 16
LAT_ROWS = KV_LORA + BF16_SUBLANES
KEY_BLOCK = 512
POOL_HALO = 8
MOD_ROWS = 16
V7X_VMEM_BYTES = 64 * 1024 * 1024
VMEM_LIMIT = V7X_VMEM_BYTES - 8 * 1024 * 1024

_C_QLAT = 0
_C_KVLAT = _C_QLAT + Q_LORA
_C_GATE_A = _C_KVLAT + KV_LORA
_C_POOL = _C_GATE_A + MLA_WIDTH
_C_GATE_B = _C_POOL + POOL_WIDTH
_C_KPE = _C_GATE_B + POOL_WIDTH
_C_END = _C_KPE + 2 * QK_ROPE

BF16 = jnp.bfloat16
F32 = jnp.float32


def _dot(a, b):
    return jnp.dot(a, b, preferred_element_type=F32)


def _dot_nt(a, b):
    return lax.dot_general(a, b, (((1,), (1,)), ((), ())), preferred_element_type=F32)


def _rms(x, g):
    return x * lax.rsqrt(jnp.mean(x * x, axis=-1, keepdims=True) + EPS) * g


def _ada_kernel(c_ref, w_ref, b_ref, o_ref):
    a = jax.nn.silu(c_ref[...]).astype(BF16)
    o_ref[0] = _dot(a, w_ref[0].astype(BF16)) + b_ref[0]


def _ada(cond, w_ada, b_ada):
    depth, d, n = w_ada.shape
    bn = 512
    return pl.pallas_call(
        _ada_kernel,
        grid=(depth, n // bn),
        in_specs=[
            pl.BlockSpec((MOD_ROWS, d), lambda l, j: (0, 0)),
            pl.BlockSpec((1, d, bn), lambda l, j: (l, 0, j)),
            pl.BlockSpec((1, 1, bn), lambda l, j: (l, 0, j)),
        ],
        out_specs=pl.BlockSpec((1, MOD_ROWS, bn), lambda l, j: (l, 0, j)),
        out_shape=jax.ShapeDtypeStruct((depth, MOD_ROWS, n), F32),
        name="ada_mod",
    )(cond, w_ada, b_ada.reshape(depth, 1, n))


def _front_kernel(*refs, tm, seq, halo, emit_cache):
    it = iter(refs)
    x_ref = next(it)
    xp_ref = next(it) if halo else None
    xn_ref = next(it) if halo else None
    mod_ref = next(it)
    npre_ref = next(it)
    win_ref = next(it)
    qn_ref = next(it)
    wq_ref = next(it)
    kvn_ref = next(it)
    wpool_ref = next(it)
    pscale_ref = next(it)
    cos_ref = next(it)
    sin_ref = next(it)
    q_ref = next(it)
    k_ref = next(it)
    lt_ref = next(it)
    ga_ref = next(it)
    pp_ref = next(it)
    ckv_ref = next(it) if emit_cache else None
    kpe_ref = next(it) if emit_cache else None

    i = pl.program_id(0)
    nt = pl.num_programs(0)
    d = D_MODEL
    shift = mod_ref[0, :, 0:d]
    scale = mod_ref[0, :, d:2 * d]
    npre = npre_ref[...]

    def modulate(xv):
        return (_rms(xv, npre) * (1.0 + scale) + shift).astype(BF16)

    def rotate(blk):
        return blk * cos_ref[...] + pltpu.roll(blk, QK_ROPE, axis=1) * sin_ref[...]

    h = modulate(x_ref[0])

    q_lat = _dot(h, win_ref[:, _C_QLAT:_C_KVLAT])
    qn = _rms(q_lat, qn_ref[...]).astype(BF16)
    qa = _dot(qn, wq_ref[:, 0:MLA_HEADS * KV_LORA]) * Q_SCALE
    qp = _dot(qn, wq_ref[:, MLA_HEADS * KV_LORA:]) * Q_SCALE
    for hd in range(MLA_HEADS):
        q_ref[0, hd, :, 0:KV_LORA] = qa[:, hd * KV_LORA:(hd + 1) * KV_LORA].astype(BF16)
        q_ref[0, hd, :, KV_LORA:QK_PAD] = rotate(
            qp[:, hd * 2 * QK_ROPE:(hd + 1) * 2 * QK_ROPE]).astype(BF16)

    ckv = _rms(_dot(h, win_ref[:, _C_KVLAT:_C_GATE_A]), kvn_ref[...])
    kpe2 = _dot(h, win_ref[:, _C_KPE:_C_END])
    if emit_cache:
        ckv_ref[0] = ckv
        kpe_ref[0] = kpe2[:, 0:QK_ROPE]
    k_ref[0, :, 0:KV_LORA] = ckv.astype(BF16)
    k_ref[0, :, KV_LORA:QK_PAD] = rotate(kpe2).astype(BF16)
    lt_ref[0, 0, 0:KV_LORA] = ckv.T.astype(BF16)
    lt_ref[0, 0, KV_LORA:LAT_ROWS] = jnp.ones((BF16_SUBLANES, tm), BF16)

    ga_ref[0] = jax.nn.silu(_dot(h, win_ref[:, _C_GATE_A:_C_POOL])).astype(BF16)

    u = _dot(h, win_ref[:, _C_POOL:_C_GATE_B])
    if halo:
        hh = modulate(jnp.concatenate([xp_ref[0], xn_ref[0]], axis=0))
        uh = _dot(hh, win_ref[:, _C_POOL:_C_GATE_B])
        up = jnp.where(i > 0, uh[0:POOL_HALO], 0.0)
        un = jnp.where(i < nt - 1, uh[POOL_HALO:2 * POOL_HALO], 0.0)
    else:
        up = jnp.zeros((POOL_HALO, POOL_WIDTH), F32)
        un = up
    ue = jnp.concatenate([up, u, un], axis=0)
    ext = tm + 2 * POOL_HALO
    t = i * tm + lax.broadcasted_iota(jnp.int32, (tm, 1), 0)
    gate_b = _dot(h, win_ref[:, _C_GATE_B:_C_KPE])
    for g, w in enumerate(POOL_WINDOWS):
        sl = slice(g * POOL_GROUP, (g + 1) * POOL_GROUP)
        p = ue[:, sl]
        k = 1
        while k < w:
            p = p + pltpu.roll(p, k, axis=0)
            k *= 2
        lead = w // 2 - 1
        if lead:
            p = pltpu.roll(p, ext - lead, axis=0)
        wsum = p[POOL_HALO:POOL_HALO + tm]
        cnt = (jnp.minimum(t + w // 2, seq) - jnp.maximum(t - w // 2, 0)).astype(F32)
        dlt = (wsum * (1.0 / cnt) - u[:, sl]).astype(BF16)
        og = _dot(dlt, wpool_ref[g]) * pscale_ref[:, sl]
        pp_ref[0, :, sl] = (og * jax.nn.silu(gate_b[:, sl])).astype(BF16)


def _front(x, mod, mod_row, npre, w_in, qn, w_q, kvn, w_pool, pscale, tables, *, tm, emit_cache):
    b, seq, d = x.shape
    nt = seq // tm
    halo = nt > 1
    hb = tm // POOL_HALO
    last = seq // POOL_HALO - 1

    def const(shape):
        return pl.BlockSpec(shape, lambda i, j: (0,) * len(shape))

    in_specs = [pl.BlockSpec((1, tm, d), lambda i, j: (j, i, 0))]
    args = [x]
    if halo:
        in_specs += [
            pl.BlockSpec((1, POOL_HALO, d), lambda i, j: (j, jnp.maximum(i * hb - 1, 0), 0)),
            pl.BlockSpec((1, POOL_HALO, d), lambda i, j: (j, jnp.minimum((i + 1) * hb, last), 0)),
        ]
        args += [x, x]
    in_specs += [
        pl.BlockSpec((1, 1, 3 * d), lambda i, j: (mod_row(j), 0, 0)),
        const((1, d)), const(w_in.shape), const((1, Q_LORA)), const(w_q.shape),
        const((1, KV_LORA)), const(w_pool.shape), const((1, POOL_WIDTH)),
        pl.BlockSpec((tm, 2 * QK_ROPE), lambda i, j: (i, 0)),
        pl.BlockSpec((tm, 2 * QK_ROPE), lambda i, j: (i, 0)),
    ]
    args += [mod, npre, w_in, qn, w_q, kvn, w_pool, pscale, *tables]

    out_specs = [
        pl.BlockSpec((1, MLA_HEADS, tm, QK_PAD), lambda i, j: (j, 0, i, 0)),
        pl.BlockSpec((1, tm, QK_PAD), lambda i, j: (j, i, 0)),
        pl.BlockSpec((1, 1, LAT_ROWS, tm), lambda i, j: (j, i, 0, 0)),
        pl.BlockSpec((1, tm, MLA_WIDTH), lambda i, j: (j, i, 0)),
        pl.BlockSpec((1, tm, POOL_WIDTH), lambda i, j: (j, i, 0)),
    ]
    out_shape = [
        jax.ShapeDtypeStruct((b, MLA_HEADS, seq, QK_PAD), BF16),
        jax.ShapeDtypeStruct((b, seq, QK_PAD), BF16),
        jax.ShapeDtypeStruct((b, nt, LAT_ROWS, tm), BF16),
        jax.ShapeDtypeStruct((b, seq, MLA_WIDTH), BF16),
        jax.ShapeDtypeStruct((b, seq, POOL_WIDTH), BF16),
    ]
    if emit_cache:
        out_specs += [
            pl.BlockSpec((1, tm, KV_LORA), lambda i, j: (j, i, 0)),
            pl.BlockSpec((1, tm, QK_ROPE), lambda i, j: (j, i, 0)),
        ]
        out_shape += [
            jax.ShapeDtypeStruct((b, seq, KV_LORA), F32),
            jax.ShapeDtypeStruct((b, seq, QK_ROPE), F32),
        ]
    return pl.pallas_call(
        functools.partial(_front_kernel, tm=tm, seq=seq, halo=halo, emit_cache=emit_cache),
        grid=(nt, b),
        in_specs=in_specs,
        out_specs=out_specs,
        out_shape=out_shape,
        compiler_params=pltpu.CompilerParams(
            dimension_semantics=("arbitrary", "arbitrary"), vmem_limit_bytes=VMEM_LIMIT),
        name="mla_pool_front",
    )(*args)


def _ctx_kernel(ckv_ref, kpe_ref, k_ref, lt_ref):
    ckv = ckv_ref[0]
    k_ref[0, :, 0:KV_LORA] = ckv.astype(BF16)
    k_ref[0, :, KV_LORA:QK_PAD] = kpe_ref[0].astype(BF16)
    lt_ref[0, 0:KV_LORA] = ckv.T.astype(BF16)
    lt_ref[0, KV_LORA:LAT_ROWS] = jnp.ones((BF16_SUBLANES, ckv.shape[0]), BF16)


def _ctx_keys(ckv, kpe_pad):
    b, past, _ = ckv.shape
    return pl.pallas_call(
        _ctx_kernel,
        grid=(b,),
        in_specs=[
            pl.BlockSpec((1, past, KV_LORA), lambda j: (j, 0, 0)),
            pl.BlockSpec((1, past, 2 * QK_ROPE), lambda j: (j, 0, 0)),
        ],
        out_specs=[
            pl.BlockSpec((1, past, QK_PAD), lambda j: (j, 0, 0)),
            pl.BlockSpec((1, LAT_ROWS, past), lambda j: (j, 0, 0)),
        ],
        out_shape=[
            jax.ShapeDtypeStruct((b, past, QK_PAD), BF16),
            jax.ShapeDtypeStruct((b, LAT_ROWS, past), BF16),
        ],
        name="ctx_keys",
    )(ckv, kpe_pad)


def _key_blocks(k_ref, lt_ref, kc_ref, ltc_ref):
    nchunks, kc = lt_ref.shape[1], lt_ref.shape[3]
    kb = min(KEY_BLOCK, kc)
    blocks = []
    for c in range(nchunks):
        for r in range(0, kc, kb):
            blocks.append((kb, k_ref.at[0, c * kc + r:c * kc + r + kb, :],
                           lt_ref.at[0, c, :, r:r + kb]))
    if kc_ref is not None:
        blocks.append((kc_ref.shape[1], kc_ref.at[0], ltc_ref.at[0]))
    return blocks


def _attention_tile(q_ref, blocks, wuvt_ref, ga_ref, pp_ref, wout_ref, mix_scr):
    tq = q_ref.shape[2]
    sub = 8

    def score_head(hd):
        m8 = None
        scores = []
        for n, kblk, _ in blocks:
            s = _dot_nt(kblk[...], q_ref[0, hd])
            scores.append(s)
            cm = jnp.max(s.reshape(n // sub, sub, tq), axis=0)
            m8 = cm if m8 is None else jnp.maximum(m8, cm)
        return scores, m8

    scores, m8 = score_head(0)
    for hd in range(MLA_HEADS):
        m = jnp.max(m8, axis=0, keepdims=True)
        current = scores
        if hd + 1 < MLA_HEADS:
            scores, m8 = score_head(hd + 1)
        else:
            out = _dot(pp_ref[0], wout_ref[MLA_WIDTH:, :])
        acc = None
        for (_, _, ltblk), s in zip(blocks, current):
            p = jnp.exp2(s - m).astype(BF16)
            part = _dot(ltblk[...], p)
            acc = part if acc is None else acc + part
        o_lat = (acc[0:KV_LORA] * (1.0 / acc[KV_LORA:KV_LORA + 1])).astype(BF16)
        o = _dot(wuvt_ref[hd], o_lat).T
        sl = slice(hd * V_DIM, (hd + 1) * V_DIM)
        mix_scr[:, sl] = (o * ga_ref[0, :, sl].astype(F32)).astype(BF16)
    return out + _dot(mix_scr[...], wout_ref[0:MLA_WIDTH, :])


def _attn_kernel(*refs, has_ctx):
    it = iter(refs)
    q_ref = next(it)
    k_ref = next(it)
    lt_ref = next(it)
    kc_ref = next(it) if has_ctx else None
    ltc_ref = next(it) if has_ctx else None
    wuvt_ref = next(it)
    ga_ref = next(it)
    pp_ref = next(it)
    x_ref = next(it)
    mod_ref = next(it)
    npost_ref = next(it)
    wout_ref = next(it)
    o_ref = next(it)
    mix_scr = next(it)

    blocks = _key_blocks(k_ref, lt_ref, kc_ref, ltc_ref)
    out = _attention_tile(q_ref, blocks, wuvt_ref, ga_ref, pp_ref, wout_ref, mix_scr)
    gate = mod_ref[0, :, 2 * D_MODEL:3 * D_MODEL]
    o_ref[0] = x_ref[0] + gate * _rms(out, npost_ref[...])


def _attend(q, k, lt, ctx, w_uvt, ga, pp, x, mod, mod_row, npost, w_out, *, tq):
    b, seq, d = x.shape
    nchunks, kc = lt.shape[1], lt.shape[3]
    has_ctx = ctx is not None

    def const(shape):
        return pl.BlockSpec(shape, lambda j, i: (0,) * len(shape))

    in_specs = [
        pl.BlockSpec((1, MLA_HEADS, tq, QK_PAD), lambda j, i: (j, 0, i, 0)),
        pl.BlockSpec((1, seq, QK_PAD), lambda j, i: (j, 0, 0)),
        pl.BlockSpec((1, nchunks, LAT_ROWS, kc), lambda j, i: (j, 0, 0, 0)),
    ]
    args = [q, k, lt]
    if has_ctx:
        kctx, ltctx = ctx
        past = kctx.shape[1]
        in_specs += [
            pl.BlockSpec((1, past, QK_PAD), lambda j, i: (j, 0, 0)),
            pl.BlockSpec((1, LAT_ROWS, past), lambda j, i: (j, 0, 0)),
        ]
        args += [kctx, ltctx]
    in_specs += [
        const(w_uvt.shape),
        pl.BlockSpec((1, tq, MLA_WIDTH), lambda j, i: (j, i, 0)),
        pl.BlockSpec((1, tq, POOL_WIDTH), lambda j, i: (j, i, 0)),
        pl.BlockSpec((1, tq, d), lambda j, i: (j, i, 0)),
        pl.BlockSpec((1, 1, 3 * d), lambda j, i: (mod_row(j), 0, 0)),
        const((1, d)), const(w_out.shape),
    ]
    args += [w_uvt, ga, pp, x, mod, npost, w_out]
    return pl.pallas_call(
        functools.partial(_attn_kernel, has_ctx=has_ctx),
        grid=(b, seq // tq),
        in_specs=in_specs,
        out_specs=pl.BlockSpec((1, tq, d), lambda j, i: (j, i, 0)),
        out_shape=jax.ShapeDtypeStruct((b, seq, d), F32),
        scratch_shapes=[pltpu.VMEM((tq, MLA_WIDTH), BF16)],
        compiler_params=pltpu.CompilerParams(
            dimension_semantics=("arbitrary", "arbitrary"), vmem_limit_bytes=VMEM_LIMIT),
        name="mla_attend_out",
    )(*args)


def _gmlp_stages(load_x, store_y, mod_ref, npre_ref, win_ref, lng_ref, lnb_ref, ws_ref, bs_ref,
                 wout_ref, npost_ref, z_scr):
    d = D_MODEL
    tm = z_scr.shape[0]
    st = {}

    def project_u():
        shift = mod_ref[0, :, 0:d]
        scale = mod_ref[0, :, d:2 * d]
        st["h"] = (_rms(load_x(), npre_ref[...]) * (1.0 + scale) + shift).astype(BF16)
        st["u"] = _dot(st["h"], win_ref[:, 0:d])

    def project_v():
        st["v"] = _dot(st["h"], win_ref[:, d:2 * d])

    def project_gate():
        st["g"] = _dot(st.pop("h"), win_ref[:, 2 * d:3 * d])

    def mix_and_project():
        v = jax.nn.gelu(st.pop("v"))
        mu = jnp.mean(v, axis=-1, keepdims=True)
        vc = v - mu
        var = jnp.mean(vc * vc, axis=-1, keepdims=True)
        vn = (vc * lax.rsqrt(var + EPS) * lng_ref[...] + lnb_ref[...]).astype(BF16)
        us = jax.nn.gelu(st.pop("u")) * jax.nn.silu(st.pop("g"))
        for n in range(tm // CHUNK):
            rows = slice(n * CHUNK, (n + 1) * CHUNK)
            for g in range(SGU_GROUPS):
                cols = slice(g * SGU_GROUP_DIM, (g + 1) * SGU_GROUP_DIM)
                sv = _dot(ws_ref[g], vn[rows, cols]) + bs_ref[:, cols]
                z_scr[rows, cols] = (us[rows, cols] * sv).astype(BF16)
        out = _dot(z_scr[...], wout_ref[...])
        gate = mod_ref[0, :, 2 * d:3 * d]
        store_y(load_x() + gate * _rms(out, npost_ref[...]))

    return [project_u, project_v, project_gate, mix_and_project]


def _gmlp_kernel(x_ref, mod_ref, npre_ref, win_ref, lng_ref, lnb_ref, ws_ref, bs_ref, wout_ref,
                 npost_ref, o_ref, z_scr):
    def store_y(y):
        o_ref[0] = y

    for stage in _gmlp_stages(lambda: x_ref[0], store_y, mod_ref, npre_ref, win_ref, lng_ref, lnb_ref,
                              ws_ref, bs_ref, wout_ref, npost_ref, z_scr):
        stage()


def _gmlp(x, mod, mod_row, npre, w_in, lng, lnb, w_s, bias, w_out, npost, *, tm):
    b, seq, d = x.shape

    def const(shape):
        return pl.BlockSpec(shape, lambda j, i: (0,) * len(shape))

    return pl.pallas_call(
        _gmlp_kernel,
        grid=(b, seq // tm),
        in_specs=[
            pl.BlockSpec((1, tm, d), lambda j, i: (j, i, 0)),
            pl.BlockSpec((1, 1, 3 * d), lambda j, i: (mod_row(j), 0, 0)),
            const((1, d)), const(w_in.shape), const((1, d)), const((1, d)),
            const(w_s.shape), const(bias.shape), const(w_out.shape), const((1, d)),
        ],
        out_specs=pl.BlockSpec((1, tm, d), lambda j, i: (j, i, 0)),
        out_shape=jax.ShapeDtypeStruct((b, seq, d), F32),
        scratch_shapes=[pltpu.VMEM((tm, d), BF16)],
        compiler_params=pltpu.CompilerParams(
            dimension_semantics=("arbitrary", "arbitrary"), vmem_limit_bytes=VMEM_LIMIT),
        name="gmlp_layer",
    )(x, mod, npre, w_in, lng, lnb, w_s, bias, w_out, npost)


def _swap16(w):
    half = QK_ROPE // 4
    parts = [w[..., k * half:(k + 1) * half] for k in range(4)]
    return jnp.concatenate([parts[1], parts[0], parts[3], parts[2]], axis=-1)


def _rope_tables(seq):
    t = np.arange(seq)
    half = QK_ROPE // 4
    inv = ROPE_THETA ** (-np.arange(half, dtype=np.float64) / half)
    ang_r = (t // GRID_W)[:, None] * inv
    ang_c = (t % GRID_W)[:, None] * inv
    cr, sr, cc, sc = np.cos(ang_r), np.sin(ang_r), np.cos(ang_c), np.sin(ang_c)
    zero = np.zeros((seq, QK_ROPE))
    cos = np.concatenate([cr, cr, cc, cc, zero], axis=-1)
    sin = np.concatenate([-sr, sr, -sc, sc, zero], axis=-1)
    return jnp.asarray(cos, F32), jnp.asarray(sin, F32)


def _no_position_tables(seq):
    one = np.concatenate([np.ones((seq, QK_ROPE)), np.zeros((seq, QK_ROPE))], axis=-1)
    return jnp.asarray(one, F32), jnp.zeros((seq, 2 * QK_ROPE), F32)


def _absorb_kernel(wqn_ref, wukt_ref, o_ref):
    for hd in range(MLA_HEADS):
        o_ref[hd] = _dot(wqn_ref[hd].astype(BF16), wukt_ref[hd].astype(BF16)).astype(BF16)


def _absorb(wq_nope, w_ukt):
    return pl.pallas_call(
        _absorb_kernel,
        out_shape=jax.ShapeDtypeStruct((MLA_HEADS, Q_LORA, KV_LORA), BF16),
        name="absorb_q",
    )(wq_nope, w_ukt)


def kernel(x_prompt, x_sample, cache_ckv, cache_kpe, c, c_ctx, w_ada, b_ada, norm_pre, norm_post,
           w_in_ap, q_norm, w_uq, kv_norm, w_ukv, w_pool, pool_scale, w_out_ap,
           w_in_c, sgu_ln_g, sgu_ln_b, w_s, b_s, w_out_c):
    d = D_MODEL
    dec_b = x_sample.shape[0]
    ctx_row = dec_b

    cond = jnp.zeros((MOD_ROWS, d), F32).at[:dec_b].set(c).at[ctx_row].set(c_ctx)
    mod = _ada(cond, w_ada, b_ada)
    mod0 = mod[0].reshape(MOD_ROWS, 1, 3 * d)
    mod1 = mod[1].reshape(MOD_ROWS, 1, 3 * d)
    row_sample = lambda j: j
    row_prompt = lambda j: ctx_row

    wi = w_in_ap[0]
    q_lat, kv_lat, k_pe, gate_a, pool_in, gate_b = jnp.split(
        wi, (Q_LORA, Q_LORA + KV_LORA, Q_LORA + KV_LORA + QK_ROPE,
             Q_LORA + KV_LORA + QK_ROPE + MLA_WIDTH,
             Q_LORA + KV_LORA + QK_ROPE + MLA_WIDTH + POOL_WIDTH), axis=1)
    w_in = jnp.concatenate(
        [q_lat, kv_lat, gate_a, pool_in, gate_b, k_pe, _swap16(k_pe)], axis=1).astype(BF16)

    wkv = w_ukv[0].reshape(KV_LORA, MLA_HEADS, QK_NOPE + V_DIM)
    w_ukt = wkv[..., :QK_NOPE].transpose(1, 2, 0)
    w_uvt = wkv[..., QK_NOPE:].transpose(1, 2, 0).astype(BF16)

    wq = w_uq[0].reshape(Q_LORA, MLA_HEADS, QK_NOPE + QK_ROPE)
    wq_n, wq_p = wq[..., :QK_NOPE], wq[..., QK_NOPE:]
    w_qa = _absorb(wq_n.transpose(1, 0, 2), w_ukt)
    w_q = jnp.concatenate(
        [w_qa.transpose(1, 0, 2).reshape(Q_LORA, -1),
         jnp.concatenate([wq_p, _swap16(wq_p)], axis=-1).reshape(Q_LORA, -1).astype(BF16)], axis=1)

    w_pool_b = w_pool[0].astype(BF16)
    w_out_b = w_out_ap[0].astype(BF16)
    npre0, npost0 = norm_pre[0][None], norm_post[0][None]
    npre1, npost1 = norm_pre[1][None], norm_post[1][None]
    qn, kvn, pscale = q_norm[0][None], kv_norm[0][None], pool_scale[0][None]

    w_in_c_b = w_in_c[0].astype(BF16)
    w_s_b = w_s[0].astype(BF16)
    bias = jnp.repeat(b_s[0].T, SGU_GROUP_DIM, axis=1)
    w_out_c_b = w_out_c[0].astype(BF16)
    lng, lnb = sgu_ln_g[0][None], sgu_ln_b[0][None]

    seq_p = x_prompt.shape[1]
    qp, kp, ltp, gap, ppp, ckv_new, kpe_new = _front(
        x_prompt, mod0, row_prompt, npre0, w_in, qn, w_q, kvn,
        w_pool_b, pscale, _no_position_tables(seq_p), tm=seq_p, emit_cache=True)
    xp1 = _attend(qp, kp, ltp, None, w_uvt, gap, ppp, x_prompt, mod0, row_prompt, npost0, w_out_b,
                  tq=seq_p)
    pair = xp1.reshape(xp1.shape[0] // 2, 2 * seq_p, d)
    y_prompt = _gmlp(pair, mod1, row_prompt, npre1, w_in_c_b, lng, lnb, w_s_b, bias, w_out_c_b, npost1,
                     tm=2 * seq_p).reshape(xp1.shape)

    seq_s = x_sample.shape[1]
    kpe_pad = jnp.pad(cache_kpe[:, 0], ((0, 0), (0, 0), (0, QK_ROPE)))
    ctx = _ctx_keys(cache_ckv[:, 0], kpe_pad)
    qs, ks, lts, gas, pps = _front(
        x_sample, mod0, row_sample, npre0, w_in, qn, w_q, kvn,
        w_pool_b, pscale, _rope_tables(seq_s), tm=1024, emit_cache=False)
    xs1 = _attend(qs, ks, lts, ctx, w_uvt, gas, pps, x_sample, mod0, row_sample, npost0, w_out_b,
                  tq=512)
    y_sample = _gmlp(xs1, mod1, row_sample, npre1, w_in_c_b, lng, lnb, w_s_b, bias, w_out_c_b, npost1,
                     tm=512)

    return (y_prompt, y_sample, ckv_new[:, None], kpe_new[:, None])
```

```python
import functools

import jax
import jax.numpy as jnp
import numpy as np
from jax import lax
from jax.experimental import pallas as pl
from jax.experimental.pallas import tpu as pltpu

D_MODEL = 1024
EPS = 1e-6
MLA_HEADS = 4
Q_LORA = 256
KV_LORA = 128
QK_NOPE = 128
QK_ROPE = 64
V_DIM = 128
ROPE_THETA = 10000.0
GRID_W = 64
POOL_WINDOWS = (2, 4, 8, 16)
POOL_GROUP = 128
POOL_WIDTH = len(POOL_WINDOWS) * POOL_GROUP
MLA_WIDTH = MLA_HEADS * V_DIM
CHUNK = 128
SGU_GROUPS = 4
SGU_GROUP_DIM = D_MODEL // SGU_GROUPS
LOG2_E = 1.4426950408889634
Q_SCALE = (QK_NOPE + QK_ROPE) ** -0.5 * LOG2_E

QK_PAD = KV_LORA + 2 * QK_ROPE
BF16_SUBLANES = 16
LAT_ROWS = KV_LORA + BF16_SUBLANES
KEY_BLOCK = 512
POOL_HALO = 8
MOD_ROWS = 16
V7X_VMEM_BYTES = 64 * 1024 * 1024
VMEM_LIMIT = V7X_VMEM_BYTES - 8 * 1024 * 1024

_C_QLAT = 0
_C_KVLAT = _C_QLAT + Q_LORA
_C_GATE_A = _C_KVLAT + KV_LORA
_C_POOL = _C_GATE_A + MLA_WIDTH
_C_GATE_B = _C_POOL + POOL_WIDTH
_C_KPE = _C_GATE_B + POOL_WIDTH
_C_END = _C_KPE + 2 * QK_ROPE

BF16 = jnp.bfloat16
F32 = jnp.float32


def _dot(a, b):
    return jnp.dot(a, b, preferred_element_type=F32)


def _dot_nt(a, b):
    return lax.dot_general(a, b, (((1,), (1,)), ((), ())), preferred_element_type=F32)


def _rms(x, g):
    return x * lax.rsqrt(jnp.mean(x * x, axis=-1, keepdims=True) + EPS) * g


def _ada_kernel(c_ref, w_ref, b_ref, o_ref):
    a = jax.nn.silu(c_ref[...]).astype(BF16)
    o_ref[0] = _dot(a, w_ref[0].astype(BF16)) + b_ref[0]


def _ada(cond, w_ada, b_ada):
    depth, d, n = w_ada.shape
    bn = 512
    return pl.pallas_call(
        _ada_kernel,
        grid=(depth, n // bn),
        in_specs=[
            pl.BlockSpec((MOD_ROWS, d), lambda l, j: (0, 0)),
            pl.BlockSpec((1, d, bn), lambda l, j: (l, 0, j)),
            pl.BlockSpec((1, 1, bn), lambda l, j: (l, 0, j)),
        ],
        out_specs=pl.BlockSpec((1, MOD_ROWS, bn), lambda l, j: (l, 0, j)),
        out_shape=jax.ShapeDtypeStruct((depth, MOD_ROWS, n), F32),
        name="ada_mod",
    )(cond, w_ada, b_ada.reshape(depth, 1, n))


def _front_kernel(*refs, tm, seq, halo, emit_cache):
    it = iter(refs)
    x_ref = next(it)
    xp_ref = next(it) if halo else None
    xn_ref = next(it) if halo else None
    mod_ref = next(it)
    npre_ref = next(it)
    win_ref = next(it)
    qn_ref = next(it)
    wq_ref = next(it)
    kvn_ref = next(it)
    wpool_ref = next(it)
    pscale_ref = next(it)
    cos_ref = next(it)
    sin_ref = next(it)
    q_ref = next(it)
    k_ref = next(it)
    lt_ref = next(it)
    ga_ref = next(it)
    pp_ref = next(it)
    ckv_ref = next(it) if emit_cache else None
    kpe_ref = next(it) if emit_cache else None

    i = pl.program_id(0)
    nt = pl.num_programs(0)
    d = D_MODEL
    shift = mod_ref[0, :, 0:d]
    scale = mod_ref[0, :, d:2 * d]
    npre = npre_ref[...]

    def modulate(xv):
        return (_rms(xv, npre) * (1.0 + scale) + shift).astype(BF16)

    def rotate(blk):
        return blk * cos_ref[...] + pltpu.roll(blk, QK_ROPE, axis=1) * sin_ref[...]

    h = modulate(x_ref[0])

    q_lat = _dot(h, win_ref[:, _C_QLAT:_C_KVLAT])
    qn = _rms(q_lat, qn_ref[...]).astype(BF16)
    qa = _dot(qn, wq_ref[:, 0:MLA_HEADS * KV_LORA]) * Q_SCALE
    qp = _dot(qn, wq_ref[:, MLA_HEADS * KV_LORA:]) * Q_SCALE
    for hd in range(MLA_HEADS):
        q_ref[0, hd, :, 0:KV_LORA] = qa[:, hd * KV_LORA:(hd + 1) * KV_LORA].astype(BF16)
        q_ref[0, hd, :, KV_LORA:QK_PAD] = rotate(
            qp[:, hd * 2 * QK_ROPE:(hd + 1) * 2 * QK_ROPE]).astype(BF16)

    ckv = _rms(_dot(h, win_ref[:, _C_KVLAT:_C_GATE_A]), kvn_ref[...])
    kpe2 = _dot(h, win_ref[:, _C_KPE:_C_END])
    if emit_cache:
        ckv_ref[0] = ckv
        kpe_ref[0] = kpe2[:, 0:QK_ROPE]
    k_ref[0, :, 0:KV_LORA] = ckv.astype(BF16)
    k_ref[0, :, KV_LORA:QK_PAD] = rotate(kpe2).astype(BF16)
    lt_ref[0, 0, 0:KV_LORA] = ckv.T.astype(BF16)
    lt_ref[0, 0, KV_LORA:LAT_ROWS] = jnp.ones((BF16_SUBLANES, tm), BF16)

    ga_ref[0] = jax.nn.silu(_dot(h, win_ref[:, _C_GATE_A:_C_POOL])).astype(BF16)

    u = _dot(h, win_ref[:, _C_POOL:_C_GATE_B])
    if halo:
        hh = modulate(jnp.concatenate([xp_ref[0], xn_ref[0]], axis=0))
        uh = _dot(hh, win_ref[:, _C_POOL:_C_GATE_B])
        up = jnp.where(i > 0, uh[0:POOL_HALO], 0.0)
        un = jnp.where(i < nt - 1, uh[POOL_HALO:2 * POOL_HALO], 0.0)
    else:
        up = jnp.zeros((POOL_HALO, POOL_WIDTH), F32)
        un = up
    ue = jnp.concatenate([up, u, un], axis=0)
    ext = tm + 2 * POOL_HALO
    t = i * tm + lax.broadcasted_iota(jnp.int32, (tm, 1), 0)
    gate_b = _dot(h, win_ref[:, _C_GATE_B:_C_KPE])
    for g, w in enumerate(POOL_WINDOWS):
        sl = slice(g * POOL_GROUP, (g + 1) * POOL_GROUP)
        p = ue[:, sl]
        k = 1
        while k < w:
            p = p + pltpu.roll(p, k, axis=0)
            k *= 2
        lead = w // 2 - 1
        if lead:
            p = pltpu.roll(p, ext - lead, axis=0)
        wsum = p[POOL_HALO:POOL_HALO + tm]
        cnt = (jnp.minimum(t + w // 2, seq) - jnp.maximum(t - w // 2, 0)).astype(F32)
        dlt = (wsum * (1.0 / cnt) - u[:, sl]).astype(BF16)
        og = _dot(dlt, wpool_ref[g]) * pscale_ref[:, sl]
        pp_ref[0, :, sl] = (og * jax.nn.silu(gate_b[:, sl])).astype(BF16)


def _front(x, mod, mod_row, npre, w_in, qn, w_q, kvn, w_pool, pscale, tables, *, tm, emit_cache):
    b, seq, d = x.shape
    nt = seq // tm
    halo = nt > 1
    hb = tm // POOL_HALO
    last = seq // POOL_HALO - 1

    def const(shape):
        return pl.BlockSpec(shape, lambda i, j: (0,) * len(shape))

    in_specs = [pl.BlockSpec((1, tm, d), lambda i, j: (j, i, 0))]
    args = [x]
    if halo:
        in_specs += [
            pl.BlockSpec((1, POOL_HALO, d), lambda i, j: (j, jnp.maximum(i * hb - 1, 0), 0)),
            pl.BlockSpec((1, POOL_HALO, d), lambda i, j: (j, jnp.minimum((i + 1) * hb, last), 0)),
        ]
        args += [x, x]
    in_specs += [
        pl.BlockSpec((1, 1, 3 * d), lambda i, j: (mod_row(j), 0, 0)),
        const((1, d)), const(w_in.shape), const((1, Q_LORA)), const(w_q.shape),
        const((1, KV_LORA)), const(w_pool.shape), const((1, POOL_WIDTH)),
        pl.BlockSpec((tm, 2 * QK_ROPE), lambda i, j: (i, 0)),
        pl.BlockSpec((tm, 2 * QK_ROPE), lambda i, j: (i, 0)),
    ]
    args += [mod, npre, w_in, qn, w_q, kvn, w_pool, pscale, *tables]

    out_specs = [
        pl.BlockSpec((1, MLA_HEADS, tm, QK_PAD), lambda i, j: (j, 0, i, 0)),
        pl.BlockSpec((1, tm, QK_PAD), lambda i, j: (j, i, 0)),
        pl.BlockSpec((1, 1, LAT_ROWS, tm), lambda i, j: (j, i, 0, 0)),
        pl.BlockSpec((1, tm, MLA_WIDTH), lambda i, j: (j, i, 0)),
        pl.BlockSpec((1, tm, POOL_WIDTH), lambda i, j: (j, i, 0)),
    ]
    out_shape = [
        jax.ShapeDtypeStruct((b, MLA_HEADS, seq, QK_PAD), BF16),
        jax.ShapeDtypeStruct((b, seq, QK_PAD), BF16),
        jax.ShapeDtypeStruct((b, nt, LAT_ROWS, tm), BF16),
        jax.ShapeDtypeStruct((b, seq, MLA_WIDTH), BF16),
        jax.ShapeDtypeStruct((b, seq, POOL_WIDTH), BF16),
    ]
    if emit_cache:
        out_specs += [
            pl.BlockSpec((1, tm, KV_LORA), lambda i, j: (j, i, 0)),
            pl.BlockSpec((1, tm, QK_ROPE), lambda i, j: (j, i, 0)),
        ]
        out_shape += [
            jax.ShapeDtypeStruct((b, seq, KV_LORA), F32),
            jax.ShapeDtypeStruct((b, seq, QK_ROPE), F32),
        ]
    return pl.pallas_call(
        functools.partial(_front_kernel, tm=tm, seq=seq, halo=halo, emit_cache=emit_cache),
        grid=(nt, b),
        in_specs=in_specs,
        out_specs=out_specs,
        out_shape=out_shape,
        compiler_params=pltpu.CompilerParams(
            dimension_semantics=("arbitrary", "arbitrary"), vmem_limit_bytes=VMEM_LIMIT),
        name="mla_pool_front",
    )(*args)


def _ctx_kernel(ckv_ref, kpe_ref, k_ref, lt_ref):
    ckv = ckv_ref[0]
    k_ref[0, :, 0:KV_LORA] = ckv.astype(BF16)
    k_ref[0, :, KV_LORA:QK_PAD] = kpe_ref[0].astype(BF16)
    lt_ref[0, 0:KV_LORA] = ckv.T.astype(BF16)
    lt_ref[0, KV_LORA:LAT_ROWS] = jnp.ones((BF16_SUBLANES, ckv.shape[0]), BF16)


def _ctx_keys(ckv, kpe_pad):
    b, past, _ = ckv.shape
    return pl.pallas_call(
        _ctx_kernel,
        grid=(b,),
        in_specs=[
            pl.BlockSpec((1, past, KV_LORA), lambda j: (j, 0, 0)),
            pl.BlockSpec((1, past, 2 * QK_ROPE), lambda j: (j, 0, 0)),
        ],
        out_specs=[
            pl.BlockSpec((1, past, QK_PAD), lambda j: (j, 0, 0)),
            pl.BlockSpec((1, LAT_ROWS, past), lambda j: (j, 0, 0)),
        ],
        out_shape=[
            jax.ShapeDtypeStruct((b, past, QK_PAD), BF16),
            jax.ShapeDtypeStruct((b, LAT_ROWS, past), BF16),
        ],
        name="ctx_keys",
    )(ckv, kpe_pad)


def _key_blocks(k_ref, lt_ref, kc_ref, ltc_ref):
    nchunks, kc = lt_ref.shape[1], lt_ref.shape[3]
    kb = min(KEY_BLOCK, kc)
    blocks = []
    for c in range(nchunks):
        for r in range(0, kc, kb):
            blocks.append((kb, k_ref.at[0, c * kc + r:c * kc + r + kb, :],
                           lt_ref.at[0, c, :, r:r + kb]))
    if kc_ref is not None:
        blocks.append((kc_ref.shape[1], kc_ref.at[0], ltc_ref.at[0]))
    return blocks


def _attention_tile(q_ref, blocks, wuvt_ref, ga_ref, pp_ref, wout_ref, mix_scr, s_bufs):
    tq = q_ref.shape[2]
    sub = 8

    zero = jnp.minimum(pl.program_id(0), 0)
    starts = []
    r0 = 0
    for n, _, _ in blocks:
        starts.append(r0)
        r0 += n

    def rows(bi):
        return pl.ds(pl.multiple_of(starts[bi] + zero, sub), blocks[bi][0])

    def score_head(hd):
        m8 = None
        for bi, (n, kblk, _) in enumerate(blocks):
            s = _dot_nt(kblk[...], q_ref[0, hd])
            s_bufs[hd % 2][rows(bi), :] = s
            cm = jnp.max(s.reshape(n // sub, sub, tq), axis=0)
            m8 = cm if m8 is None else jnp.maximum(m8, cm)
        return m8

    m8 = score_head(0)
    for hd in range(MLA_HEADS):
        m = jnp.max(m8, axis=0, keepdims=True)
        if hd + 1 < MLA_HEADS:
            m8 = score_head(hd + 1)
        else:
            out = _dot(pp_ref[0], wout_ref[MLA_WIDTH:, :])
        acc = None
        for bi, (_, _, ltblk) in enumerate(blocks):
            p = jnp.exp2(s_bufs[hd % 2][rows(bi), :] - m).astype(BF16)
            part = _dot(ltblk[...], p)
            acc = part if acc is None else acc + part
        o_lat = (acc[0:KV_LORA] * (1.0 / acc[KV_LORA:KV_LORA + 1])).astype(BF16)
        o = _dot(wuvt_ref[hd], o_lat).T
        sl = slice(hd * V_DIM, (hd + 1) * V_DIM)
        mix_scr[:, sl] = (o * ga_ref[0, :, sl].astype(F32)).astype(BF16)
    return out + _dot(mix_scr[...], wout_ref[0:MLA_WIDTH, :])


def _attn_kernel(*refs, has_ctx):
    it = iter(refs)
    q_ref = next(it)
    k_ref = next(it)
    lt_ref = next(it)
    kc_ref = next(it) if has_ctx else None
    ltc_ref = next(it) if has_ctx else None
    wuvt_ref = next(it)
    ga_ref = next(it)
    pp_ref = next(it)
    x_ref = next(it)
    mod_ref = next(it)
    npost_ref = next(it)
    wout_ref = next(it)
    o_ref = next(it)
    mix_scr = next(it)
    s_bufs = (next(it), next(it))

    blocks = _key_blocks(k_ref, lt_ref, kc_ref, ltc_ref)
    out = _attention_tile(q_ref, blocks, wuvt_ref, ga_ref, pp_ref, wout_ref, mix_scr, s_bufs)
    gate = mod_ref[0, :, 2 * D_MODEL:3 * D_MODEL]
    o_ref[0] = x_ref[0] + gate * _rms(out, npost_ref[...])


def _attend(q, k, lt, ctx, w_uvt, ga, pp, x, mod, mod_row, npost, w_out, *, tq):
    b, seq, d = x.shape
    nchunks, kc = lt.shape[1], lt.shape[3]
    has_ctx = ctx is not None
    n_keys = seq + (ctx[0].shape[1] if has_ctx else 0)

    def const(shape):
        return pl.BlockSpec(shape, lambda j, i: (0,) * len(shape))

    in_specs = [
        pl.BlockSpec((1, MLA_HEADS, tq, QK_PAD), lambda j, i: (j, 0, i, 0)),
        pl.BlockSpec((1, seq, QK_PAD), lambda j, i: (j, 0, 0)),
        pl.BlockSpec((1, nchunks, LAT_ROWS, kc), lambda j, i: (j, 0, 0, 0)),
    ]
    args = [q, k, lt]
    if has_ctx:
        kctx, ltctx = ctx
        past = kctx.shape[1]
        in_specs += [
            pl.BlockSpec((1, past, QK_PAD), lambda j, i: (j, 0, 0)),
            pl.BlockSpec((1, LAT_ROWS, past), lambda j, i: (j, 0, 0)),
        ]
        args += [kctx, ltctx]
    in_specs += [
        const(w_uvt.shape),
        pl.BlockSpec((1, tq, MLA_WIDTH), lambda j, i: (j, i, 0)),
        pl.BlockSpec((1, tq, POOL_WIDTH), lambda j, i: (j, i, 0)),
        pl.BlockSpec((1, tq, d), lambda j, i: (j, i, 0)),
        pl.BlockSpec((1, 1, 3 * d), lambda j, i: (mod_row(j), 0, 0)),
        const((1, d)), const(w_out.shape),
    ]
    args += [w_uvt, ga, pp, x, mod, npost, w_out]
    return pl.pallas_call(
        functools.partial(_attn_kernel, has_ctx=has_ctx),
        grid=(b, seq // tq),
        in_specs=in_specs,
        out_specs=pl.BlockSpec((1, tq, d), lambda j, i: (j, i, 0)),
        out_shape=jax.ShapeDtypeStruct((b, seq, d), F32),
        scratch_shapes=[pltpu.VMEM((tq, MLA_WIDTH), BF16),
                        pltpu.VMEM((n_keys, tq), F32), pltpu.VMEM((n_keys, tq), F32)],
        compiler_params=pltpu.CompilerParams(
            dimension_semantics=("arbitrary", "arbitrary"), vmem_limit_bytes=VMEM_LIMIT),
        name="mla_attend_out",
    )(*args)


def _gmlp_stages(load_x, store_y, mod_ref, npre_ref, win_ref, lng_ref, lnb_ref, ws_ref, bs_ref,
                 wout_ref, npost_ref, z_scr):
    d = D_MODEL
    tm = z_scr.shape[0]
    st = {}

    def project_u():
        shift = mod_ref[0, :, 0:d]
        scale = mod_ref[0, :, d:2 * d]
        st["h"] = (_rms(load_x(), npre_ref[...]) * (1.0 + scale) + shift).astype(BF16)
        st["u"] = _dot(st["h"], win_ref[:, 0:d])

    def project_v():
        st["v"] = _dot(st["h"], win_ref[:, d:2 * d])

    def project_gate():
        st["g"] = _dot(st.pop("h"), win_ref[:, 2 * d:3 * d])

    def mix_and_project():
        v = jax.nn.gelu(st.pop("v"))
        mu = jnp.mean(v, axis=-1, keepdims=True)
        vc = v - mu
        var = jnp.mean(vc * vc, axis=-1, keepdims=True)
        vn = (vc * lax.rsqrt(var + EPS) * lng_ref[...] + lnb_ref[...]).astype(BF16)
        us = jax.nn.gelu(st.pop("u")) * jax.nn.silu(st.pop("g"))
        for n in range(tm // CHUNK):
            rows = slice(n * CHUNK, (n + 1) * CHUNK)
            for g in range(SGU_GROUPS):
                cols = slice(g * SGU_GROUP_DIM, (g + 1) * SGU_GROUP_DIM)
                sv = _dot(ws_ref[g], vn[rows, cols]) + bs_ref[:, cols]
                z_scr[rows, cols] = (us[rows, cols] * sv).astype(BF16)
        out = _dot(z_scr[...], wout_ref[...])
        gate = mod_ref[0, :, 2 * d:3 * d]
        store_y(load_x() + gate * _rms(out, npost_ref[...]))

    return [project_u, project_v, project_gate, mix_and_project]


def _gmlp_kernel(x_ref, mod_ref, npre_ref, win_ref, lng_ref, lnb_ref, ws_ref, bs_ref, wout_ref,
                 npost_ref, o_ref, z_scr):
    def store_y(y):
        o_ref[0] = y

    for stage in _gmlp_stages(lambda: x_ref[0], store_y, mod_ref, npre_ref, win_ref, lng_ref, lnb_ref,
                              ws_ref, bs_ref, wout_ref, npost_ref, z_scr):
        stage()


def _gmlp(x, mod, mod_row, npre, w_in, lng, lnb, w_s, bias, w_out, npost, *, tm):
    b, seq, d = x.shape

    def const(shape):
        return pl.BlockSpec(shape, lambda j, i: (0,) * len(shape))

    return pl.pallas_call(
        _gmlp_kernel,
        grid=(b, seq // tm),
        in_specs=[
            pl.BlockSpec((1, tm, d), lambda j, i: (j, i, 0)),
            pl.BlockSpec((1, 1, 3 * d), lambda j, i: (mod_row(j), 0, 0)),
            const((1, d)), const(w_in.shape), const((1, d)), const((1, d)),
            const(w_s.shape), const(bias.shape), const(w_out.shape), const((1, d)),
        ],
        out_specs=pl.BlockSpec((1, tm, d), lambda j, i: (j, i, 0)),
        out_shape=jax.ShapeDtypeStruct((b, seq, d), F32),
        scratch_shapes=[pltpu.VMEM((tm, d), BF16)],
        compiler_params=pltpu.CompilerParams(
            dimension_semantics=("arbitrary", "arbitrary"), vmem_limit_bytes=VMEM_LIMIT),
        name="gmlp_layer",
    )(x, mod, npre, w_in, lng, lnb, w_s, bias, w_out, npost)


def _swap16(w):
    half = QK_ROPE // 4
    parts = [w[..., k * half:(k + 1) * half] for k in range(4)]
    return jnp.concatenate([parts[1], parts[0], parts[3], parts[2]], axis=-1)


def _rope_tables(seq):
    t = np.arange(seq)
    half = QK_ROPE // 4
    inv = ROPE_THETA ** (-np.arange(half, dtype=np.float64) / half)
    ang_r = (t // GRID_W)[:, None] * inv
    ang_c = (t % GRID_W)[:, None] * inv
    cr, sr, cc, sc = np.cos(ang_r), np.sin(ang_r), np.cos(ang_c), np.sin(ang_c)
    zero = np.zeros((seq, QK_ROPE))
    cos = np.concatenate([cr, cr, cc, cc, zero], axis=-1)
    sin = np.concatenate([-sr, sr, -sc, sc, zero], axis=-1)
    return jnp.asarray(cos, F32), jnp.asarray(sin, F32)


def _no_position_tables(seq):
    one = np.concatenate([np.ones((seq, QK_ROPE)), np.zeros((seq, QK_ROPE))], axis=-1)
    return jnp.asarray(one, F32), jnp.zeros((seq, 2 * QK_ROPE), F32)


def _absorb_kernel(wqn_ref, wukt_ref, o_ref):
    for hd in range(MLA_HEADS):
        o_ref[hd] = _dot(wqn_ref[hd].astype(BF16), wukt_ref[hd].astype(BF16)).astype(BF16)


def _absorb(wq_nope, w_ukt):
    return pl.pallas_call(
        _absorb_kernel,
        out_shape=jax.ShapeDtypeStruct((MLA_HEADS, Q_LORA, KV_LORA), BF16),
        name="absorb_q",
    )(wq_nope, w_ukt)


def kernel(x_prompt, x_sample, cache_ckv, cache_kpe, c, c_ctx, w_ada, b_ada, norm_pre, norm_post,
           w_in_ap, q_norm, w_uq, kv_norm, w_ukv, w_pool, pool_scale, w_out_ap,
           w_in_c, sgu_ln_g, sgu_ln_b, w_s, b_s, w_out_c):
    d = D_MODEL
    dec_b = x_sample.shape[0]
    ctx_row = dec_b

    cond = jnp.zeros((MOD_ROWS, d), F32).at[:dec_b].set(c).at[ctx_row].set(c_ctx)
    mod = _ada(cond, w_ada, b_ada)
    mod0 = mod[0].reshape(MOD_ROWS, 1, 3 * d)
    mod1 = mod[1].reshape(MOD_ROWS, 1, 3 * d)
    row_sample = lambda j: j
    row_prompt = lambda j: ctx_row

    wi = w_in_ap[0]
    q_lat, kv_lat, k_pe, gate_a, pool_in, gate_b = jnp.split(
        wi, (Q_LORA, Q_LORA + KV_LORA, Q_LORA + KV_LORA + QK_ROPE,
             Q_LORA + KV_LORA + QK_ROPE + MLA_WIDTH,
             Q_LORA + KV_LORA + QK_ROPE + MLA_WIDTH + POOL_WIDTH), axis=1)
    w_in = jnp.concatenate(
        [q_lat, kv_lat, gate_a, pool_in, gate_b, k_pe, _swap16(k_pe)], axis=1).astype(BF16)

    wkv = w_ukv[0].reshape(KV_LORA, MLA_HEADS, QK_NOPE + V_DIM)
    w_ukt = wkv[..., :QK_NOPE].transpose(1, 2, 0)
    w_uvt = wkv[..., QK_NOPE:].transpose(1, 2, 0).astype(BF16)

    wq = w_uq[0].reshape(Q_LORA, MLA_HEADS, QK_NOPE + QK_ROPE)
    wq_n, wq_p = wq[..., :QK_NOPE], wq[..., QK_NOPE:]
    w_qa = _absorb(wq_n.transpose(1, 0, 2), w_ukt)
    w_q = jnp.concatenate(
        [w_qa.transpose(1, 0, 2).reshape(Q_LORA, -1),
         jnp.concatenate([wq_p, _swap16(wq_p)], axis=-1).reshape(Q_LORA, -1).astype(BF16)], axis=1)

    w_pool_b = w_pool[0].astype(BF16)
    w_out_b = w_out_ap[0].astype(BF16)
    npre0, npost0 = norm_pre[0][None], norm_post[0][None]
    npre1, npost1 = norm_pre[1][None], norm_post[1][None]
    qn, kvn, pscale = q_norm[0][None], kv_norm[0][None], pool_scale[0][None]

    w_in_c_b = w_in_c[0].astype(BF16)
    w_s_b = w_s[0].astype(BF16)
    bias = jnp.repeat(b_s[0].T, SGU_GROUP_DIM, axis=1)
    w_out_c_b = w_out_c[0].astype(BF16)
    lng, lnb = sgu_ln_g[0][None], sgu_ln_b[0][None]

    seq_p = x_prompt.shape[1]
    qp, kp, ltp, gap, ppp, ckv_new, kpe_new = _front(
        x_prompt, mod0, row_prompt, npre0, w_in, qn, w_q, kvn,
        w_pool_b, pscale, _no_position_tables(seq_p), tm=seq_p, emit_cache=True)
    xp1 = _attend(qp, kp, ltp, None, w_uvt, gap, ppp, x_prompt, mod0, row_prompt, npost0, w_out_b,
                  tq=seq_p)
    pair = xp1.reshape(xp1.shape[0] // 2, 2 * seq_p, d)
    y_prompt = _gmlp(pair, mod1, row_prompt, npre1, w_in_c_b, lng, lnb, w_s_b, bias, w_out_c_b, npost1,
                     tm=2 * seq_p).reshape(xp1.shape)

    seq_s = x_sample.shape[1]
    kpe_pad = jnp.pad(cache_kpe[:, 0], ((0, 0), (0, 0), (0, QK_ROPE)))
    ctx = _ctx_keys(cache_ckv[:, 0], kpe_pad)
    qs, ks, lts, gas, pps = _front(
        x_sample, mod0, row_sample, npre0, w_in, qn, w_q, kvn,
        w_pool_b, pscale, _rope_tables(seq_s), tm=1024, emit_cache=False)
    xs1 = _attend(qs, ks, lts, ctx, w_uvt, gas, pps, x_sample, mod0, row_sample, npost0, w_out_b,
                  tq=512)
    y_sample = _gmlp(xs1, mod1, row_sample, npre1, w_in_c_b, lng, lnb, w_s_b, bias, w_out_c_b, npost1,
                     tm=512)

    return (y_prompt, y_sample, ckv_new[:, None], kpe_new[:, None])
```

```python
import functools

import jax
import jax.numpy as jnp
import numpy as np
from jax import lax
from jax.experimental import pallas as pl
from jax.experimental.pallas import tpu as pltpu

D_MODEL = 1024
EPS = 1e-6
MLA_HEADS = 4
Q_LORA = 256
KV_LORA = 128
QK_NOPE = 128
QK_ROPE = 64
V_DIM = 128
ROPE_THETA = 10000.0
GRID_W = 64
POOL_WINDOWS = (2, 4, 8, 16)
POOL_GROUP = 128
POOL_WIDTH = len(POOL_WINDOWS) * POOL_GROUP
MLA_WIDTH = MLA_HEADS * V_DIM
CHUNK = 128
SGU_GROUPS = 4
SGU_GROUP_DIM = D_MODEL // SGU_GROUPS
LOG2_E = 1.4426950408889634
Q_SCALE = (QK_NOPE + QK_ROPE) ** -0.5 * LOG2_E

QK_PAD = KV_LORA + 2 * QK_ROPE
BF16_SUBLANES = 16
LAT_ROWS = KV_LORA + BF16_SUBLANES
KEY_BLOCK = 1024
POOL_HALO = 8
MOD_ROWS = 16
V7X_VMEM_BYTES = 64 * 1024 * 1024
VMEM_LIMIT = V7X_VMEM_BYTES - 8 * 1024 * 1024

_C_QLAT = 0
_C_KVLAT = _C_QLAT + Q_LORA
_C_GATE_A = _C_KVLAT + KV_LORA
_C_POOL = _C_GATE_A + MLA_WIDTH
_C_GATE_B = _C_POOL + POOL_WIDTH
_C_KPE = _C_GATE_B + POOL_WIDTH
_C_END = _C_KPE + 2 * QK_ROPE

BF16 = jnp.bfloat16
F32 = jnp.float32


def _dot(a, b):
    return jnp.dot(a, b, preferred_element_type=F32)


def _dot_nt(a, b):
    return lax.dot_general(a, b, (((1,), (1,)), ((), ())), preferred_element_type=F32)


def _rms(x, g):
    return x * lax.rsqrt(jnp.mean(x * x, axis=-1, keepdims=True) + EPS) * g


def _ada_kernel(c_ref, w_ref, b_ref, o_ref):
    a = jax.nn.silu(c_ref[...]).astype(BF16)
    o_ref[0] = _dot(a, w_ref[0].astype(BF16)) + b_ref[0]


def _ada(cond, w_ada, b_ada):
    depth, d, n = w_ada.shape
    bn = 512
    return pl.pallas_call(
        _ada_kernel,
        grid=(depth, n // bn),
        in_specs=[
            pl.BlockSpec((MOD_ROWS, d), lambda l, j: (0, 0)),
            pl.BlockSpec((1, d, bn), lambda l, j: (l, 0, j)),
            pl.BlockSpec((1, 1, bn), lambda l, j: (l, 0, j)),
        ],
        out_specs=pl.BlockSpec((1, MOD_ROWS, bn), lambda l, j: (l, 0, j)),
        out_shape=jax.ShapeDtypeStruct((depth, MOD_ROWS, n), F32),
        name="ada_mod",
    )(cond, w_ada, b_ada.reshape(depth, 1, n))


def _front_kernel(*refs, tm, seq, halo, emit_cache):
    it = iter(refs)
    x_ref = next(it)
    xp_ref = next(it) if halo else None
    xn_ref = next(it) if halo else None
    mod_ref = next(it)
    npre_ref = next(it)
    win_ref = next(it)
    qn_ref = next(it)
    wq_ref = next(it)
    kvn_ref = next(it)
    wpool_ref = next(it)
    pscale_ref = next(it)
    cos_ref = next(it)
    sin_ref = next(it)
    q_ref = next(it)
    k_ref = next(it)
    lt_ref = next(it)
    ga_ref = next(it)
    pp_ref = next(it)
    ckv_ref = next(it) if emit_cache else None
    kpe_ref = next(it) if emit_cache else None

    i = pl.program_id(0)
    nt = pl.num_programs(0)
    d = D_MODEL
    shift = mod_ref[0, :, 0:d]
    scale = mod_ref[0, :, d:2 * d]
    npre = npre_ref[...]

    def modulate(xv):
        return (_rms(xv, npre) * (1.0 + scale) + shift).astype(BF16)

    def rotate(blk):
        return blk * cos_ref[...] + pltpu.roll(blk, QK_ROPE, axis=1) * sin_ref[...]

    h = modulate(x_ref[0])

    q_lat = _dot(h, win_ref[:, _C_QLAT:_C_KVLAT])
    qn = _rms(q_lat, qn_ref[...]).astype(BF16)
    qa = _dot(qn, wq_ref[:, 0:MLA_HEADS * KV_LORA]) * Q_SCALE
    qp = _dot(qn, wq_ref[:, MLA_HEADS * KV_LORA:]) * Q_SCALE
    for hd in range(MLA_HEADS):
        q_ref[0, hd, :, 0:KV_LORA] = qa[:, hd * KV_LORA:(hd + 1) * KV_LORA].astype(BF16)
        q_ref[0, hd, :, KV_LORA:QK_PAD] = rotate(
            qp[:, hd * 2 * QK_ROPE:(hd + 1) * 2 * QK_ROPE]).astype(BF16)

    ckv = _rms(_dot(h, win_ref[:, _C_KVLAT:_C_GATE_A]), kvn_ref[...])
    kpe2 = _dot(h, win_ref[:, _C_KPE:_C_END])
    if emit_cache:
        ckv_ref[0] = ckv
        kpe_ref[0] = kpe2[:, 0:QK_ROPE]
    k_ref[0, :, 0:KV_LORA] = ckv.astype(BF16)
    k_ref[0, :, KV_LORA:QK_PAD] = rotate(kpe2).astype(BF16)
    lt_ref[0, 0, 0:KV_LORA] = ckv.T.astype(BF16)
    lt_ref[0, 0, KV_LORA:LAT_ROWS] = jnp.ones((BF16_SUBLANES, tm), BF16)

    ga_ref[0] = jax.nn.silu(_dot(h, win_ref[:, _C_GATE_A:_C_POOL])).astype(BF16)

    u = _dot(h, win_ref[:, _C_POOL:_C_GATE_B])
    if halo:
        hh = modulate(jnp.concatenate([xp_ref[0], xn_ref[0]], axis=0))
        uh = _dot(hh, win_ref[:, _C_POOL:_C_GATE_B])
        up = jnp.where(i > 0, uh[0:POOL_HALO], 0.0)
        un = jnp.where(i < nt - 1, uh[POOL_HALO:2 * POOL_HALO], 0.0)
    else:
        up = jnp.zeros((POOL_HALO, POOL_WIDTH), F32)
        un = up
    ue = jnp.concatenate([up, u, un], axis=0)
    ext = tm + 2 * POOL_HALO
    t = i * tm + lax.broadcasted_iota(jnp.int32, (tm, 1), 0)
    gate_b = _dot(h, win_ref[:, _C_GATE_B:_C_KPE])
    for g, w in enumerate(POOL_WINDOWS):
        sl = slice(g * POOL_GROUP, (g + 1) * POOL_GROUP)
        p = ue[:, sl]
        k = 1
        while k < w:
            p = p + pltpu.roll(p, k, axis=0)
            k *= 2
        lead = w // 2 - 1
        if lead:
            p = pltpu.roll(p, ext - lead, axis=0)
        wsum = p[POOL_HALO:POOL_HALO + tm]
        cnt = (jnp.minimum(t + w // 2, seq) - jnp.maximum(t - w // 2, 0)).astype(F32)
        dlt = (wsum * (1.0 / cnt) - u[:, sl]).astype(BF16)
        og = _dot(dlt, wpool_ref[g]) * pscale_ref[:, sl]
        pp_ref[0, :, sl] = (og * jax.nn.silu(gate_b[:, sl])).astype(BF16)


def _front(x, mod, mod_row, npre, w_in, qn, w_q, kvn, w_pool, pscale, tables, *, tm, emit_cache):
    b, seq, d = x.shape
    nt = seq // tm
    halo = nt > 1
    hb = tm // POOL_HALO
    last = seq // POOL_HALO - 1

    def const(shape):
        return pl.BlockSpec(shape, lambda i, j: (0,) * len(shape))

    in_specs = [pl.BlockSpec((1, tm, d), lambda i, j: (j, i, 0))]
    args = [x]
    if halo:
        in_specs += [
            pl.BlockSpec((1, POOL_HALO, d), lambda i, j: (j, jnp.maximum(i * hb - 1, 0), 0)),
            pl.BlockSpec((1, POOL_HALO, d), lambda i, j: (j, jnp.minimum((i + 1) * hb, last), 0)),
        ]
        args += [x, x]
    in_specs += [
        pl.BlockSpec((1, 1, 3 * d), lambda i, j: (mod_row(j), 0, 0)),
        const((1, d)), const(w_in.shape), const((1, Q_LORA)), const(w_q.shape),
        const((1, KV_LORA)), const(w_pool.shape), const((1, POOL_WIDTH)),
        pl.BlockSpec((tm, 2 * QK_ROPE), lambda i, j: (i, 0)),
        pl.BlockSpec((tm, 2 * QK_ROPE), lambda i, j: (i, 0)),
    ]
    args += [mod, npre, w_in, qn, w_q, kvn, w_pool, pscale, *tables]

    out_specs = [
        pl.BlockSpec((1, MLA_HEADS, tm, QK_PAD), lambda i, j: (j, 0, i, 0)),
        pl.BlockSpec((1, tm, QK_PAD), lambda i, j: (j, i, 0)),
        pl.BlockSpec((1, 1, LAT_ROWS, tm), lambda i, j: (j, i, 0, 0)),
        pl.BlockSpec((1, tm, MLA_WIDTH), lambda i, j: (j, i, 0)),
        pl.BlockSpec((1, tm, POOL_WIDTH), lambda i, j: (j, i, 0)),
    ]
    out_shape = [
        jax.ShapeDtypeStruct((b, MLA_HEADS, seq, QK_PAD), BF16),
        jax.ShapeDtypeStruct((b, seq, QK_PAD), BF16),
        jax.ShapeDtypeStruct((b, nt, LAT_ROWS, tm), BF16),
        jax.ShapeDtypeStruct((b, seq, MLA_WIDTH), BF16),
        jax.ShapeDtypeStruct((b, seq, POOL_WIDTH), BF16),
    ]
    if emit_cache:
        out_specs += [
            pl.BlockSpec((1, tm, KV_LORA), lambda i, j: (j, i, 0)),
            pl.BlockSpec((1, tm, QK_ROPE), lambda i, j: (j, i, 0)),
        ]
        out_shape += [
            jax.ShapeDtypeStruct((b, seq, KV_LORA), F32),
            jax.ShapeDtypeStruct((b, seq, QK_ROPE), F32),
        ]
    return pl.pallas_call(
        functools.partial(_front_kernel, tm=tm, seq=seq, halo=halo, emit_cache=emit_cache),
        grid=(nt, b),
        in_specs=in_specs,
        out_specs=out_specs,
        out_shape=out_shape,
        compiler_params=pltpu.CompilerParams(
            dimension_semantics=("arbitrary", "arbitrary"), vmem_limit_bytes=VMEM_LIMIT),
        name="mla_pool_front",
    )(*args)


def _ctx_kernel(ckv_ref, kpe_ref, k_ref, lt_ref):
    ckv = ckv_ref[0]
    k_ref[0, :, 0:KV_LORA] = ckv.astype(BF16)
    k_ref[0, :, KV_LORA:QK_PAD] = kpe_ref[0].astype(BF16)
    lt_ref[0, 0:KV_LORA] = ckv.T.astype(BF16)
    lt_ref[0, KV_LORA:LAT_ROWS] = jnp.ones((BF16_SUBLANES, ckv.shape[0]), BF16)


def _ctx_keys(ckv, kpe_pad):
    b, past, _ = ckv.shape
    return pl.pallas_call(
        _ctx_kernel,
        grid=(b,),
        in_specs=[
            pl.BlockSpec((1, past, KV_LORA), lambda j: (j, 0, 0)),
            pl.BlockSpec((1, past, 2 * QK_ROPE), lambda j: (j, 0, 0)),
        ],
        out_specs=[
            pl.BlockSpec((1, past, QK_PAD), lambda j: (j, 0, 0)),
            pl.BlockSpec((1, LAT_ROWS, past), lambda j: (j, 0, 0)),
        ],
        out_shape=[
            jax.ShapeDtypeStruct((b, past, QK_PAD), BF16),
            jax.ShapeDtypeStruct((b, LAT_ROWS, past), BF16),
        ],
        name="ctx_keys",
    )(ckv, kpe_pad)


def _key_blocks(k_ref, lt_ref, kc_ref, ltc_ref):
    nchunks, kc = lt_ref.shape[1], lt_ref.shape[3]
    kb = min(KEY_BLOCK, kc)
    blocks = []
    for c in range(nchunks):
        for r in range(0, kc, kb):
            blocks.append((kb, k_ref.at[0, c * kc + r:c * kc + r + kb, :],
                           lt_ref.at[0, c, :, r:r + kb]))
    if kc_ref is not None:
        blocks.append((kc_ref.shape[1], kc_ref.at[0], ltc_ref.at[0]))
    return blocks


def _attention_tile(q_ref, blocks, wuvt_ref, ga_ref, pp_ref, wout_ref, mix_scr, s_bufs):
    tq = q_ref.shape[2]
    sub = 8

    zero = jnp.minimum(pl.program_id(0), 0)
    starts = []
    r0 = 0
    for n, _, _ in blocks:
        starts.append(r0)
        r0 += n

    def rows(bi):
        return pl.ds(pl.multiple_of(starts[bi] + zero, sub), blocks[bi][0])

    def score_head(hd):
        m8 = None
        for bi, (n, kblk, _) in enumerate(blocks):
            s = _dot_nt(kblk[...], q_ref[0, hd])
            s_bufs[hd % 2][rows(bi), :] = s
            cm = jnp.max(s.reshape(n // sub, sub, tq), axis=0)
            m8 = cm if m8 is None else jnp.maximum(m8, cm)
        return m8

    m8 = score_head(0)
    for hd in range(MLA_HEADS):
        m = jnp.max(m8, axis=0, keepdims=True)
        if hd + 1 < MLA_HEADS:
            m8 = score_head(hd + 1)
        else:
            out = _dot(pp_ref[0], wout_ref[MLA_WIDTH:, :])
        acc = None
        for bi, (_, _, ltblk) in enumerate(blocks):
            p = jnp.exp2(s_bufs[hd % 2][rows(bi), :] - m).astype(BF16)
            part = _dot(ltblk[...], p)
            acc = part if acc is None else acc + part
        o_lat = (acc[0:KV_LORA] * (1.0 / acc[KV_LORA:KV_LORA + 1])).astype(BF16)
        o = _dot(wuvt_ref[hd], o_lat).T
        sl = slice(hd * V_DIM, (hd + 1) * V_DIM)
        mix_scr[:, sl] = (o * ga_ref[0, :, sl].astype(F32)).astype(BF16)
    return out + _dot(mix_scr[...], wout_ref[0:MLA_WIDTH, :])


def _attn_kernel(*refs, has_ctx):
    it = iter(refs)
    q_ref = next(it)
    k_ref = next(it)
    lt_ref = next(it)
    kc_ref = next(it) if has_ctx else None
    ltc_ref = next(it) if has_ctx else None
    wuvt_ref = next(it)
    ga_ref = next(it)
    pp_ref = next(it)
    x_ref = next(it)
    mod_ref = next(it)
    npost_ref = next(it)
    wout_ref = next(it)
    o_ref = next(it)
    mix_scr = next(it)
    s_bufs = (next(it), next(it))

    blocks = _key_blocks(k_ref, lt_ref, kc_ref, ltc_ref)
    out = _attention_tile(q_ref, blocks, wuvt_ref, ga_ref, pp_ref, wout_ref, mix_scr, s_bufs)
    gate = mod_ref[0, :, 2 * D_MODEL:3 * D_MODEL]
    o_ref[0] = x_ref[0] + gate * _rms(out, npost_ref[...])


def _attend(q, k, lt, ctx, w_uvt, ga, pp, x, mod, mod_row, npost, w_out, *, tq):
    b, seq, d = x.shape
    nchunks, kc = lt.shape[1], lt.shape[3]
    has_ctx = ctx is not None
    n_keys = seq + (ctx[0].shape[1] if has_ctx else 0)

    def const(shape):
        return pl.BlockSpec(shape, lambda j, i: (0,) * len(shape))

    in_specs = [
        pl.BlockSpec((1, MLA_HEADS, tq, QK_PAD), lambda j, i: (j, 0, i, 0)),
        pl.BlockSpec((1, seq, QK_PAD), lambda j, i: (j, 0, 0)),
        pl.BlockSpec((1, nchunks, LAT_ROWS, kc), lambda j, i: (j, 0, 0, 0)),
    ]
    args = [q, k, lt]
    if has_ctx:
        kctx, ltctx = ctx
        past = kctx.shape[1]
        in_specs += [
            pl.BlockSpec((1, past, QK_PAD), lambda j, i: (j, 0, 0)),
            pl.BlockSpec((1, LAT_ROWS, past), lambda j, i: (j, 0, 0)),
        ]
        args += [kctx, ltctx]
    in_specs += [
        const(w_uvt.shape),
        pl.BlockSpec((1, tq, MLA_WIDTH), lambda j, i: (j, i, 0)),
        pl.BlockSpec((1, tq, POOL_WIDTH), lambda j, i: (j, i, 0)),
        pl.BlockSpec((1, tq, d), lambda j, i: (j, i, 0)),
        pl.BlockSpec((1, 1, 3 * d), lambda j, i: (mod_row(j), 0, 0)),
        const((1, d)), const(w_out.shape),
    ]
    args += [w_uvt, ga, pp, x, mod, npost, w_out]
    return pl.pallas_call(
        functools.partial(_attn_kernel, has_ctx=has_ctx),
        grid=(b, seq // tq),
        in_specs=in_specs,
        out_specs=pl.BlockSpec((1, tq, d), lambda j, i: (j, i, 0)),
        out_shape=jax.ShapeDtypeStruct((b, seq, d), F32),
        scratch_shapes=[pltpu.VMEM((tq, MLA_WIDTH), BF16),
                        pltpu.VMEM((n_keys, tq), F32), pltpu.VMEM((n_keys, tq), F32)],
        compiler_params=pltpu.CompilerParams(
            dimension_semantics=("arbitrary", "arbitrary"), vmem_limit_bytes=VMEM_LIMIT),
        name="mla_attend_out",
    )(*args)


def _gmlp_stages(load_x, store_y, mod_ref, npre_ref, win_ref, lng_ref, lnb_ref, ws_ref, bs_ref,
                 wout_ref, npost_ref, z_scr):
    d = D_MODEL
    tm = z_scr.shape[0]
    st = {}

    def project_u():
        shift = mod_ref[0, :, 0:d]
        scale = mod_ref[0, :, d:2 * d]
        st["h"] = (_rms(load_x(), npre_ref[...]) * (1.0 + scale) + shift).astype(BF16)
        st["u"] = _dot(st["h"], win_ref[:, 0:d])

    def project_v():
        st["v"] = _dot(st["h"], win_ref[:, d:2 * d])

    def project_gate():
        st["g"] = _dot(st.pop("h"), win_ref[:, 2 * d:3 * d])

    def mix_and_project():
        v = jax.nn.gelu(st.pop("v"))
        mu = jnp.mean(v, axis=-1, keepdims=True)
        vc = v - mu
        var = jnp.mean(vc * vc, axis=-1, keepdims=True)
        vn = (vc * lax.rsqrt(var + EPS) * lng_ref[...] + lnb_ref[...]).astype(BF16)
        us = jax.nn.gelu(st.pop("u")) * jax.nn.silu(st.pop("g"))
        for n in range(tm // CHUNK):
            rows = slice(n * CHUNK, (n + 1) * CHUNK)
            for g in range(SGU_GROUPS):
                cols = slice(g * SGU_GROUP_DIM, (g + 1) * SGU_GROUP_DIM)
                sv = _dot(ws_ref[g], vn[rows, cols]) + bs_ref[:, cols]
                z_scr[rows, cols] = (us[rows, cols] * sv).astype(BF16)
        out = _dot(z_scr[...], wout_ref[...])
        gate = mod_ref[0, :, 2 * d:3 * d]
        store_y(load_x() + gate * _rms(out, npost_ref[...]))

    return [project_u, project_v, project_gate, mix_and_project]


def _gmlp_kernel(x_ref, mod_ref, npre_ref, win_ref, lng_ref, lnb_ref, ws_ref, bs_ref, wout_ref,
                 npost_ref, o_ref, z_scr):
    def store_y(y):
        o_ref[0] = y

    for stage in _gmlp_stages(lambda: x_ref[0], store_y, mod_ref, npre_ref, win_ref, lng_ref, lnb_ref,
                              ws_ref, bs_ref, wout_ref, npost_ref, z_scr):
        stage()


def _gmlp(x, mod, mod_row, npre, w_in, lng, lnb, w_s, bias, w_out, npost, *, tm):
    b, seq, d = x.shape

    def const(shape):
        return pl.BlockSpec(shape, lambda j, i: (0,) * len(shape))

    return pl.pallas_call(
        _gmlp_kernel,
        grid=(b, seq // tm),
        in_specs=[
            pl.BlockSpec((1, tm, d), lambda j, i: (j, i, 0)),
            pl.BlockSpec((1, 1, 3 * d), lambda j, i: (mod_row(j), 0, 0)),
            const((1, d)), const(w_in.shape), const((1, d)), const((1, d)),
            const(w_s.shape), const(bias.shape), const(w_out.shape), const((1, d)),
        ],
        out_specs=pl.BlockSpec((1, tm, d), lambda j, i: (j, i, 0)),
        out_shape=jax.ShapeDtypeStruct((b, seq, d), F32),
        scratch_shapes=[pltpu.VMEM((tm, d), BF16)],
        compiler_params=pltpu.CompilerParams(
            dimension_semantics=("arbitrary", "arbitrary"), vmem_limit_bytes=VMEM_LIMIT),
        name="gmlp_layer",
    )(x, mod, npre, w_in, lng, lnb, w_s, bias, w_out, npost)


def _swap16(w):
    half = QK_ROPE // 4
    parts = [w[..., k * half:(k + 1) * half] for k in range(4)]
    return jnp.concatenate([parts[1], parts[0], parts[3], parts[2]], axis=-1)


def _rope_tables(seq):
    t = np.arange(seq)
    half = QK_ROPE // 4
    inv = ROPE_THETA ** (-np.arange(half, dtype=np.float64) / half)
    ang_r = (t // GRID_W)[:, None] * inv
    ang_c = (t % GRID_W)[:, None] * inv
    cr, sr, cc, sc = np.cos(ang_r), np.sin(ang_r), np.cos(ang_c), np.sin(ang_c)
    zero = np.zeros((seq, QK_ROPE))
    cos = np.concatenate([cr, cr, cc, cc, zero], axis=-1)
    sin = np.concatenate([-sr, sr, -sc, sc, zero], axis=-1)
    return jnp.asarray(cos, F32), jnp.asarray(sin, F32)


def _no_position_tables(seq):
    one = np.concatenate([np.ones((seq, QK_ROPE)), np.zeros((seq, QK_ROPE))], axis=-1)
    return jnp.asarray(one, F32), jnp.zeros((seq, 2 * QK_ROPE), F32)


def _absorb_kernel(wqn_ref, wukt_ref, o_ref):
    for hd in range(MLA_HEADS):
        o_ref[hd] = _dot(wqn_ref[hd].astype(BF16), wukt_ref[hd].astype(BF16)).astype(BF16)


def _absorb(wq_nope, w_ukt):
    return pl.pallas_call(
        _absorb_kernel,
        out_shape=jax.ShapeDtypeStruct((MLA_HEADS, Q_LORA, KV_LORA), BF16),
        name="absorb_q",
    )(wq_nope, w_ukt)


def kernel(x_prompt, x_sample, cache_ckv, cache_kpe, c, c_ctx, w_ada, b_ada, norm_pre, norm_post,
           w_in_ap, q_norm, w_uq, kv_norm, w_ukv, w_pool, pool_scale, w_out_ap,
           w_in_c, sgu_ln_g, sgu_ln_b, w_s, b_s, w_out_c):
    d = D_MODEL
    dec_b = x_sample.shape[0]
    ctx_row = dec_b

    cond = jnp.zeros((MOD_ROWS, d), F32).at[:dec_b].set(c).at[ctx_row].set(c_ctx)
    mod = _ada(cond, w_ada, b_ada)
    mod0 = mod[0].reshape(MOD_ROWS, 1, 3 * d)
    mod1 = mod[1].reshape(MOD_ROWS, 1, 3 * d)
    row_sample = lambda j: j
    row_prompt = lambda j: ctx_row

    wi = w_in_ap[0]
    q_lat, kv_lat, k_pe, gate_a, pool_in, gate_b = jnp.split(
        wi, (Q_LORA, Q_LORA + KV_LORA, Q_LORA + KV_LORA + QK_ROPE,
             Q_LORA + KV_LORA + QK_ROPE + MLA_WIDTH,
             Q_LORA + KV_LORA + QK_ROPE + MLA_WIDTH + POOL_WIDTH), axis=1)
    w_in = jnp.concatenate(
        [q_lat, kv_lat, gate_a, pool_in, gate_b, k_pe, _swap16(k_pe)], axis=1).astype(BF16)

    wkv = w_ukv[0].reshape(KV_LORA, MLA_HEADS, QK_NOPE + V_DIM)
    w_ukt = wkv[..., :QK_NOPE].transpose(1, 2, 0)
    w_uvt = wkv[..., QK_NOPE:].transpose(1, 2, 0).astype(BF16)

    wq = w_uq[0].reshape(Q_LORA, MLA_HEADS, QK_NOPE + QK_ROPE)
    wq_n, wq_p = wq[..., :QK_NOPE], wq[..., QK_NOPE:]
    w_qa = _absorb(wq_n.transpose(1, 0, 2), w_ukt)
    w_q = jnp.concatenate(
        [w_qa.transpose(1, 0, 2).reshape(Q_LORA, -1),
         jnp.concatenate([wq_p, _swap16(wq_p)], axis=-1).reshape(Q_LORA, -1).astype(BF16)], axis=1)

    w_pool_b = w_pool[0].astype(BF16)
    w_out_b = w_out_ap[0].astype(BF16)
    npre0, npost0 = norm_pre[0][None], norm_post[0][None]
    npre1, npost1 = norm_pre[1][None], norm_post[1][None]
    qn, kvn, pscale = q_norm[0][None], kv_norm[0][None], pool_scale[0][None]

    w_in_c_b = w_in_c[0].astype(BF16)
    w_s_b = w_s[0].astype(BF16)
    bias = jnp.repeat(b_s[0].T, SGU_GROUP_DIM, axis=1)
    w_out_c_b = w_out_c[0].astype(BF16)
    lng, lnb = sgu_ln_g[0][None], sgu_ln_b[0][None]

    seq_p = x_prompt.shape[1]
    qp, kp, ltp, gap, ppp, ckv_new, kpe_new = _front(
        x_prompt, mod0, row_prompt, npre0, w_in, qn, w_q, kvn,
        w_pool_b, pscale, _no_position_tables(seq_p), tm=seq_p, emit_cache=True)
    xp1 = _attend(qp, kp, ltp, None, w_uvt, gap, ppp, x_prompt, mod0, row_prompt, npost0, w_out_b,
                  tq=seq_p)
    pair = xp1.reshape(xp1.shape[0] // 2, 2 * seq_p, d)
    y_prompt = _gmlp(pair, mod1, row_prompt, npre1, w_in_c_b, lng, lnb, w_s_b, bias, w_out_c_b, npost1,
                     tm=2 * seq_p).reshape(xp1.shape)

    seq_s = x_sample.shape[1]
    kpe_pad = jnp.pad(cache_kpe[:, 0], ((0, 0), (0, 0), (0, QK_ROPE)))
    ctx = _ctx_keys(cache_ckv[:, 0], kpe_pad)
    qs, ks, lts, gas, pps = _front(
        x_sample, mod0, row_sample, npre0, w_in, qn, w_q, kvn,
        w_pool_b, pscale, _rope_tables(seq_s), tm=1024, emit_cache=False)
    xs1 = _attend(qs, ks, lts, ctx, w_uvt, gas, pps, x_sample, mod0, row_sample, npost0, w_out_b,
                  tq=512)
    y_sample = _gmlp(xs1, mod1, row_sample, npre1, w_in_c_b, lng, lnb, w_s_b, bias, w_out_c_b, npost1,
                     tm=512)

    return (y_prompt, y_sample, ckv_new[:, None], kpe_new[:, None])
```

```python
import functools

import jax
import jax.numpy as jnp
import numpy as np
from jax import lax
from jax.experimental import pallas as pl
from jax.experimental.pallas import tpu as pltpu

D_MODEL = 1024
EPS = 1e-6
MLA_HEADS = 4
Q_LORA = 256
KV_LORA = 128
QK_NOPE = 128
QK_ROPE = 64
V_DIM = 128
ROPE_THETA = 10000.0
GRID_W = 64
POOL_WINDOWS = (2, 4, 8, 16)
POOL_GROUP = 128
POOL_WIDTH = len(POOL_WINDOWS) * POOL_GROUP
MLA_WIDTH = MLA_HEADS * V_DIM
CHUNK = 128
SGU_GROUPS = 4
SGU_GROUP_DIM = D_MODEL // SGU_GROUPS
LOG2_E = 1.4426950408889634
Q_SCALE = (QK_NOPE + QK_ROPE) ** -0.5 * LOG2_E

QK_PAD = KV_LORA + 2 * QK_ROPE
BF16_SUBLANES = 16
LAT_ROWS = KV_LORA + BF16_SUBLANES
KEY_BLOCK = 1024
POOL_HALO = 8
MOD_ROWS = 16
V7X_VMEM_BYTES = 64 * 1024 * 1024
VMEM_LIMIT = V7X_VMEM_BYTES - 8 * 1024 * 1024

_C_QLAT = 0
_C_KVLAT = _C_QLAT + Q_LORA
_C_GATE_A = _C_KVLAT + KV_LORA
_C_POOL = _C_GATE_A + MLA_WIDTH
_C_GATE_B = _C_POOL + POOL_WIDTH
_C_KPE = _C_GATE_B + POOL_WIDTH
_C_END = _C_KPE + 2 * QK_ROPE

BF16 = jnp.bfloat16
F32 = jnp.float32


def _dot(a, b):
    return jnp.dot(a, b, preferred_element_type=F32)


def _dot_nt(a, b):
    return lax.dot_general(a, b, (((1,), (1,)), ((), ())), preferred_element_type=F32)


def _gelu_tanh(x):
    c = np.sqrt(2.0 / np.pi)
    hx = 0.5 * x
    return hx + hx * jnp.tanh(x * (c + (c * 0.044715) * (x * x)))


def _rms(x, g):
    return x * lax.rsqrt(jnp.mean(x * x, axis=-1, keepdims=True) + EPS) * g


def _ada_kernel(c_ref, w_ref, b_ref, o_ref):
    a = jax.nn.silu(c_ref[...]).astype(BF16)
    o_ref[0] = _dot(a, w_ref[0].astype(BF16)) + b_ref[0]


def _ada(cond, w_ada, b_ada):
    depth, d, n = w_ada.shape
    bn = 512
    return pl.pallas_call(
        _ada_kernel,
        grid=(depth, n // bn),
        in_specs=[
            pl.BlockSpec((MOD_ROWS, d), lambda l, j: (0, 0)),
            pl.BlockSpec((1, d, bn), lambda l, j: (l, 0, j)),
            pl.BlockSpec((1, 1, bn), lambda l, j: (l, 0, j)),
        ],
        out_specs=pl.BlockSpec((1, MOD_ROWS, bn), lambda l, j: (l, 0, j)),
        out_shape=jax.ShapeDtypeStruct((depth, MOD_ROWS, n), F32),
        name="ada_mod",
    )(cond, w_ada, b_ada.reshape(depth, 1, n))


def _front_kernel(*refs, tm, seq, halo, emit_cache):
    it = iter(refs)
    x_ref = next(it)
    xp_ref = next(it) if halo else None
    xn_ref = next(it) if halo else None
    mod_ref = next(it)
    npre_ref = next(it)
    win_ref = next(it)
    qn_ref = next(it)
    wq_ref = next(it)
    kvn_ref = next(it)
    wpool_ref = next(it)
    pscale_ref = next(it)
    cos_ref = next(it)
    sin_ref = next(it)
    q_ref = next(it)
    k_ref = next(it)
    lt_ref = next(it)
    ga_ref = next(it)
    pp_ref = next(it)
    ckv_ref = next(it) if emit_cache else None
    kpe_ref = next(it) if emit_cache else None

    i = pl.program_id(0)
    nt = pl.num_programs(0)
    d = D_MODEL
    shift = mod_ref[0, :, 0:d]
    scale = mod_ref[0, :, d:2 * d]
    gain = npre_ref[...] * (1.0 + scale)

    def modulate(xv):
        return (_rms(xv, gain) + shift).astype(BF16)

    def rotate(blk):
        return blk * cos_ref[...] + pltpu.roll(blk, QK_ROPE, axis=1) * sin_ref[...]

    h = modulate(x_ref[0])

    q_lat = _dot(h, win_ref[:, _C_QLAT:_C_KVLAT])
    qn = _rms(q_lat, qn_ref[...]).astype(BF16)
    qa = _dot(qn, wq_ref[:, 0:MLA_HEADS * KV_LORA]) * Q_SCALE
    qp = _dot(qn, wq_ref[:, MLA_HEADS * KV_LORA:]) * Q_SCALE
    for hd in range(MLA_HEADS):
        q_ref[0, hd, :, 0:KV_LORA] = qa[:, hd * KV_LORA:(hd + 1) * KV_LORA].astype(BF16)
        q_ref[0, hd, :, KV_LORA:QK_PAD] = rotate(
            qp[:, hd * 2 * QK_ROPE:(hd + 1) * 2 * QK_ROPE]).astype(BF16)

    ckv = _rms(_dot(h, win_ref[:, _C_KVLAT:_C_GATE_A]), kvn_ref[...])
    kpe2 = _dot(h, win_ref[:, _C_KPE:_C_END])
    if emit_cache:
        ckv_ref[0] = ckv
        kpe_ref[0] = kpe2[:, 0:QK_ROPE]
    k_ref[0, :, 0:KV_LORA] = ckv.astype(BF16)
    k_ref[0, :, KV_LORA:QK_PAD] = rotate(kpe2).astype(BF16)
    lt_ref[0, 0, 0:KV_LORA] = ckv.T.astype(BF16)
    lt_ref[0, 0, KV_LORA:LAT_ROWS] = jnp.ones((BF16_SUBLANES, tm), BF16)

    ga_ref[0] = jax.nn.silu(_dot(h, win_ref[:, _C_GATE_A:_C_POOL])).astype(BF16)

    u = _dot(h, win_ref[:, _C_POOL:_C_GATE_B])
    if halo:
        hh = modulate(jnp.concatenate([xp_ref[0], xn_ref[0]], axis=0))
        uh = _dot(hh, win_ref[:, _C_POOL:_C_GATE_B])
        up = jnp.where(i > 0, uh[0:POOL_HALO], 0.0)
        un = jnp.where(i < nt - 1, uh[POOL_HALO:2 * POOL_HALO], 0.0)
    else:
        up = jnp.zeros((POOL_HALO, POOL_WIDTH), F32)
        un = up
    ue = jnp.concatenate([up, u, un], axis=0)
    ext = tm + 2 * POOL_HALO
    t = i * tm + lax.broadcasted_iota(jnp.int32, (tm, 1), 0)
    gate_b = _dot(h, win_ref[:, _C_GATE_B:_C_KPE])
    for g, w in enumerate(POOL_WINDOWS):
        sl = slice(g * POOL_GROUP, (g + 1) * POOL_GROUP)
        p = ue[:, sl]
        k = 1
        while k < w:
            p = p + pltpu.roll(p, k, axis=0)
            k *= 2
        lead = w // 2 - 1
        if lead:
            p = pltpu.roll(p, ext - lead, axis=0)
        wsum = p[POOL_HALO:POOL_HALO + tm]
        cnt = (jnp.minimum(t + w // 2, seq) - jnp.maximum(t - w // 2, 0)).astype(F32)
        dlt = (wsum * (1.0 / cnt) - u[:, sl]).astype(BF16)
        og = _dot(dlt, wpool_ref[g]) * pscale_ref[:, sl]
        pp_ref[0, :, sl] = (og * jax.nn.silu(gate_b[:, sl])).astype(BF16)


def _front(x, mod, mod_row, npre, w_in, qn, w_q, kvn, w_pool, pscale, tables, *, tm, emit_cache):
    b, seq, d = x.shape
    nt = seq // tm
    halo = nt > 1
    hb = tm // POOL_HALO
    last = seq // POOL_HALO - 1

    def const(shape):
        return pl.BlockSpec(shape, lambda i, j: (0,) * len(shape))

    in_specs = [pl.BlockSpec((1, tm, d), lambda i, j: (j, i, 0))]
    args = [x]
    if halo:
        in_specs += [
            pl.BlockSpec((1, POOL_HALO, d), lambda i, j: (j, jnp.maximum(i * hb - 1, 0), 0)),
            pl.BlockSpec((1, POOL_HALO, d), lambda i, j: (j, jnp.minimum((i + 1) * hb, last), 0)),
        ]
        args += [x, x]
    in_specs += [
        pl.BlockSpec((1, 1, 3 * d), lambda i, j: (mod_row(j), 0, 0)),
        const((1, d)), const(w_in.shape), const((1, Q_LORA)), const(w_q.shape),
        const((1, KV_LORA)), const(w_pool.shape), const((1, POOL_WIDTH)),
        pl.BlockSpec((tm, 2 * QK_ROPE), lambda i, j: (i, 0)),
        pl.BlockSpec((tm, 2 * QK_ROPE), lambda i, j: (i, 0)),
    ]
    args += [mod, npre, w_in, qn, w_q, kvn, w_pool, pscale, *tables]

    out_specs = [
        pl.BlockSpec((1, MLA_HEADS, tm, QK_PAD), lambda i, j: (j, 0, i, 0)),
        pl.BlockSpec((1, tm, QK_PAD), lambda i, j: (j, i, 0)),
        pl.BlockSpec((1, 1, LAT_ROWS, tm), lambda i, j: (j, i, 0, 0)),
        pl.BlockSpec((1, tm, MLA_WIDTH), lambda i, j: (j, i, 0)),
        pl.BlockSpec((1, tm, POOL_WIDTH), lambda i, j: (j, i, 0)),
    ]
    out_shape = [
        jax.ShapeDtypeStruct((b, MLA_HEADS, seq, QK_PAD), BF16),
        jax.ShapeDtypeStruct((b, seq, QK_PAD), BF16),
        jax.ShapeDtypeStruct((b, nt, LAT_ROWS, tm), BF16),
        jax.ShapeDtypeStruct((b, seq, MLA_WIDTH), BF16),
        jax.ShapeDtypeStruct((b, seq, POOL_WIDTH), BF16),
    ]
    if emit_cache:
        out_specs += [
            pl.BlockSpec((1, tm, KV_LORA), lambda i, j: (j, i, 0)),
            pl.BlockSpec((1, tm, QK_ROPE), lambda i, j: (j, i, 0)),
        ]
        out_shape += [
            jax.ShapeDtypeStruct((b, seq, KV_LORA), F32),
            jax.ShapeDtypeStruct((b, seq, QK_ROPE), F32),
        ]
    return pl.pallas_call(
        functools.partial(_front_kernel, tm=tm, seq=seq, halo=halo, emit_cache=emit_cache),
        grid=(nt, b),
        in_specs=in_specs,
        out_specs=out_specs,
        out_shape=out_shape,
        compiler_params=pltpu.CompilerParams(
            dimension_semantics=("arbitrary", "arbitrary"), vmem_limit_bytes=VMEM_LIMIT),
        name="mla_pool_front",
    )(*args)


def _ctx_kernel(ckv_ref, kpe_ref, k_ref, lt_ref):
    ckv = ckv_ref[0]
    k_ref[0, :, 0:KV_LORA] = ckv.astype(BF16)
    k_ref[0, :, KV_LORA:QK_PAD] = kpe_ref[0].astype(BF16)
    lt_ref[0, 0:KV_LORA] = ckv.T.astype(BF16)
    lt_ref[0, KV_LORA:LAT_ROWS] = jnp.ones((BF16_SUBLANES, ckv.shape[0]), BF16)


def _ctx_keys(ckv, kpe_pad):
    b, past, _ = ckv.shape
    return pl.pallas_call(
        _ctx_kernel,
        grid=(b,),
        in_specs=[
            pl.BlockSpec((1, past, KV_LORA), lambda j: (j, 0, 0)),
            pl.BlockSpec((1, past, 2 * QK_ROPE), lambda j: (j, 0, 0)),
        ],
        out_specs=[
            pl.BlockSpec((1, past, QK_PAD), lambda j: (j, 0, 0)),
            pl.BlockSpec((1, LAT_ROWS, past), lambda j: (j, 0, 0)),
        ],
        out_shape=[
            jax.ShapeDtypeStruct((b, past, QK_PAD), BF16),
            jax.ShapeDtypeStruct((b, LAT_ROWS, past), BF16),
        ],
        name="ctx_keys",
    )(ckv, kpe_pad)


def _key_blocks(k_ref, lt_ref, kc_ref, ltc_ref):
    nchunks, kc = lt_ref.shape[1], lt_ref.shape[3]
    kb = min(KEY_BLOCK, kc)
    blocks = []
    for c in range(nchunks):
        for r in range(0, kc, kb):
            blocks.append((kb, k_ref.at[0, c * kc + r:c * kc + r + kb, :],
                           lt_ref.at[0, c, :, r:r + kb]))
    if kc_ref is not None:
        blocks.append((kc_ref.shape[1], kc_ref.at[0], ltc_ref.at[0]))
    return blocks


def _attention_tile(q_ref, blocks, wuvt_ref, ga_ref, pp_ref, wout_ref, mix_scr, s_bufs):
    tq = q_ref.shape[2]
    sub = 8

    zero = jnp.minimum(pl.program_id(0), 0)
    starts = []
    r0 = 0
    for n, _, _ in blocks:
        starts.append(r0)
        r0 += n

    def rows(bi):
        return pl.ds(pl.multiple_of(starts[bi] + zero, sub), blocks[bi][0])

    def score_head(hd):
        m8 = None
        for bi, (n, kblk, _) in enumerate(blocks):
            s = _dot_nt(kblk[...], q_ref[0, hd])
            s_bufs[hd % 2][rows(bi), :] = s
            cm = jnp.max(s.reshape(n // sub, sub, tq), axis=0)
            m8 = cm if m8 is None else jnp.maximum(m8, cm)
        return m8

    m8 = score_head(0)
    for hd in range(MLA_HEADS):
        m = jnp.max(m8, axis=0, keepdims=True)
        if hd + 1 < MLA_HEADS:
            m8 = score_head(hd + 1)
        else:
            out = _dot(pp_ref[0], wout_ref[MLA_WIDTH:, :])
        acc = None
        for bi, (_, _, ltblk) in enumerate(blocks):
            p = jnp.exp2(s_bufs[hd % 2][rows(bi), :] - m).astype(BF16)
            part = _dot(ltblk[...], p)
            acc = part if acc is None else acc + part
        o_lat = (acc[0:KV_LORA] * (1.0 / acc[KV_LORA:KV_LORA + 1])).astype(BF16)
        o = _dot(wuvt_ref[hd], o_lat).T
        sl = slice(hd * V_DIM, (hd + 1) * V_DIM)
        mix_scr[:, sl] = (o * ga_ref[0, :, sl].astype(F32)).astype(BF16)
    return out + _dot(mix_scr[...], wout_ref[0:MLA_WIDTH, :])


def _attn_kernel(*refs, has_ctx):
    it = iter(refs)
    q_ref = next(it)
    k_ref = next(it)
    lt_ref = next(it)
    kc_ref = next(it) if has_ctx else None
    ltc_ref = next(it) if has_ctx else None
    wuvt_ref = next(it)
    ga_ref = next(it)
    pp_ref = next(it)
    x_ref = next(it)
    mod_ref = next(it)
    npost_ref = next(it)
    wout_ref = next(it)
    o_ref = next(it)
    mix_scr = next(it)
    s_bufs = (next(it), next(it))

    blocks = _key_blocks(k_ref, lt_ref, kc_ref, ltc_ref)
    out = _attention_tile(q_ref, blocks, wuvt_ref, ga_ref, pp_ref, wout_ref, mix_scr, s_bufs)
    gate = mod_ref[0, :, 2 * D_MODEL:3 * D_MODEL]
    o_ref[0] = x_ref[0] + _rms(out, npost_ref[...] * gate)


def _attend(q, k, lt, ctx, w_uvt, ga, pp, x, mod, mod_row, npost, w_out, *, tq):
    b, seq, d = x.shape
    nchunks, kc = lt.shape[1], lt.shape[3]
    has_ctx = ctx is not None
    n_keys = seq + (ctx[0].shape[1] if has_ctx else 0)

    def const(shape):
        return pl.BlockSpec(shape, lambda j, i: (0,) * len(shape))

    in_specs = [
        pl.BlockSpec((1, MLA_HEADS, tq, QK_PAD), lambda j, i: (j, 0, i, 0)),
        pl.BlockSpec((1, seq, QK_PAD), lambda j, i: (j, 0, 0)),
        pl.BlockSpec((1, nchunks, LAT_ROWS, kc), lambda j, i: (j, 0, 0, 0)),
    ]
    args = [q, k, lt]
    if has_ctx:
        kctx, ltctx = ctx
        past = kctx.shape[1]
        in_specs += [
            pl.BlockSpec((1, past, QK_PAD), lambda j, i: (j, 0, 0)),
            pl.BlockSpec((1, LAT_ROWS, past), lambda j, i: (j, 0, 0)),
        ]
        args += [kctx, ltctx]
    in_specs += [
        const(w_uvt.shape),
        pl.BlockSpec((1, tq, MLA_WIDTH), lambda j, i: (j, i, 0)),
        pl.BlockSpec((1, tq, POOL_WIDTH), lambda j, i: (j, i, 0)),
        pl.BlockSpec((1, tq, d), lambda j, i: (j, i, 0)),
        pl.BlockSpec((1, 1, 3 * d), lambda j, i: (mod_row(j), 0, 0)),
        const((1, d)), const(w_out.shape),
    ]
    args += [w_uvt, ga, pp, x, mod, npost, w_out]
    return pl.pallas_call(
        functools.partial(_attn_kernel, has_ctx=has_ctx),
        grid=(b, seq // tq),
        in_specs=in_specs,
        out_specs=pl.BlockSpec((1, tq, d), lambda j, i: (j, i, 0)),
        out_shape=jax.ShapeDtypeStruct((b, seq, d), F32),
        scratch_shapes=[pltpu.VMEM((tq, MLA_WIDTH), BF16),
                        pltpu.VMEM((n_keys, tq), F32), pltpu.VMEM((n_keys, tq), F32)],
        compiler_params=pltpu.CompilerParams(
            dimension_semantics=("arbitrary", "arbitrary"), vmem_limit_bytes=VMEM_LIMIT),
        name="mla_attend_out",
    )(*args)


def _gmlp_stages(load_x, store_y, mod_ref, npre_ref, win_ref, lng_ref, lnb_ref, ws_ref, bs_ref,
                 wout_ref, npost_ref, z_scr):
    d = D_MODEL
    tm = z_scr.shape[0]
    st = {}

    def project_u():
        shift = mod_ref[0, :, 0:d]
        scale = mod_ref[0, :, d:2 * d]
        st["h"] = (_rms(load_x(), npre_ref[...] * (1.0 + scale)) + shift).astype(BF16)
        st["u"] = _dot(st["h"], win_ref[:, 0:d])

    def project_v():
        st["v"] = _dot(st["h"], win_ref[:, d:2 * d])

    def project_gate():
        st["g"] = _dot(st.pop("h"), win_ref[:, 2 * d:3 * d])

    def mix_and_project():
        v = _gelu_tanh(st.pop("v"))
        mu = jnp.mean(v, axis=-1, keepdims=True)
        vc = v - mu
        var = jnp.mean(vc * vc, axis=-1, keepdims=True)
        vn = (vc * lax.rsqrt(var + EPS) * lng_ref[...] + lnb_ref[...]).astype(BF16)
        us = _gelu_tanh(st.pop("u")) * jax.nn.silu(st.pop("g"))
        for n in range(tm // CHUNK):
            rows = slice(n * CHUNK, (n + 1) * CHUNK)
            for g in range(SGU_GROUPS):
                cols = slice(g * SGU_GROUP_DIM, (g + 1) * SGU_GROUP_DIM)
                sv = _dot(ws_ref[g], vn[rows, cols]) + bs_ref[:, cols]
                z_scr[rows, cols] = (us[rows, cols] * sv).astype(BF16)
        out = _dot(z_scr[...], wout_ref[...])
        gate = mod_ref[0, :, 2 * d:3 * d]
        store_y(load_x() + _rms(out, npost_ref[...] * gate))

    return [project_u, project_v, project_gate, mix_and_project]


def _gmlp_kernel(x_ref, mod_ref, npre_ref, win_ref, lng_ref, lnb_ref, ws_ref, bs_ref, wout_ref,
                 npost_ref, o_ref, z_scr):
    def store_y(y):
        o_ref[0] = y

    for stage in _gmlp_stages(lambda: x_ref[0], store_y, mod_ref, npre_ref, win_ref, lng_ref, lnb_ref,
                              ws_ref, bs_ref, wout_ref, npost_ref, z_scr):
        stage()


def _gmlp(x, mod, mod_row, npre, w_in, lng, lnb, w_s, bias, w_out, npost, *, tm):
    b, seq, d = x.shape

    def const(shape):
        return pl.BlockSpec(shape, lambda j, i: (0,) * len(shape))

    return pl.pallas_call(
        _gmlp_kernel,
        grid=(b, seq // tm),
        in_specs=[
            pl.BlockSpec((1, tm, d), lambda j, i: (j, i, 0)),
            pl.BlockSpec((1, 1, 3 * d), lambda j, i: (mod_row(j), 0, 0)),
            const((1, d)), const(w_in.shape), const((1, d)), const((1, d)),
            const(w_s.shape), const(bias.shape), const(w_out.shape), const((1, d)),
        ],
        out_specs=pl.BlockSpec((1, tm, d), lambda j, i: (j, i, 0)),
        out_shape=jax.ShapeDtypeStruct((b, seq, d), F32),
        scratch_shapes=[pltpu.VMEM((tm, d), BF16)],
        compiler_params=pltpu.CompilerParams(
            dimension_semantics=("arbitrary", "arbitrary"), vmem_limit_bytes=VMEM_LIMIT),
        name="gmlp_layer",
    )(x, mod, npre, w_in, lng, lnb, w_s, bias, w_out, npost)


def _swap16(w):
    half = QK_ROPE // 4
    parts = [w[..., k * half:(k + 1) * half] for k in range(4)]
    return jnp.concatenate([parts[1], parts[0], parts[3], parts[2]], axis=-1)


def _rope_tables(seq):
    t = np.arange(seq)
    half = QK_ROPE // 4
    inv = ROPE_THETA ** (-np.arange(half, dtype=np.float64) / half)
    ang_r = (t // GRID_W)[:, None] * inv
    ang_c = (t % GRID_W)[:, None] * inv
    cr, sr, cc, sc = np.cos(ang_r), np.sin(ang_r), np.cos(ang_c), np.sin(ang_c)
    zero = np.zeros((seq, QK_ROPE))
    cos = np.concatenate([cr, cr, cc, cc, zero], axis=-1)
    sin = np.concatenate([-sr, sr, -sc, sc, zero], axis=-1)
    return jnp.asarray(cos, F32), jnp.asarray(sin, F32)


def _no_position_tables(seq):
    one = np.concatenate([np.ones((seq, QK_ROPE)), np.zeros((seq, QK_ROPE))], axis=-1)
    return jnp.asarray(one, F32), jnp.zeros((seq, 2 * QK_ROPE), F32)


def _absorb_kernel(wqn_ref, wukt_ref, o_ref):
    for hd in range(MLA_HEADS):
        o_ref[hd] = _dot(wqn_ref[hd].astype(BF16), wukt_ref[hd].astype(BF16)).astype(BF16)


def _absorb(wq_nope, w_ukt):
    return pl.pallas_call(
        _absorb_kernel,
        out_shape=jax.ShapeDtypeStruct((MLA_HEADS, Q_LORA, KV_LORA), BF16),
        name="absorb_q",
    )(wq_nope, w_ukt)


def kernel(x_prompt, x_sample, cache_ckv, cache_kpe, c, c_ctx, w_ada, b_ada, norm_pre, norm_post,
           w_in_ap, q_norm, w_uq, kv_norm, w_ukv, w_pool, pool_scale, w_out_ap,
           w_in_c, sgu_ln_g, sgu_ln_b, w_s, b_s, w_out_c):
    d = D_MODEL
    dec_b = x_sample.shape[0]
    ctx_row = dec_b

    cond = jnp.zeros((MOD_ROWS, d), F32).at[:dec_b].set(c).at[ctx_row].set(c_ctx)
    mod = _ada(cond, w_ada, b_ada)
    mod0 = mod[0].reshape(MOD_ROWS, 1, 3 * d)
    mod1 = mod[1].reshape(MOD_ROWS, 1, 3 * d)
    row_sample = lambda j: j
    row_prompt = lambda j: ctx_row

    wi = w_in_ap[0]
    q_lat, kv_lat, k_pe, gate_a, pool_in, gate_b = jnp.split(
        wi, (Q_LORA, Q_LORA + KV_LORA, Q_LORA + KV_LORA + QK_ROPE,
             Q_LORA + KV_LORA + QK_ROPE + MLA_WIDTH,
             Q_LORA + KV_LORA + QK_ROPE + MLA_WIDTH + POOL_WIDTH), axis=1)
    w_in = jnp.concatenate(
        [q_lat, kv_lat, gate_a, pool_in, gate_b, k_pe, _swap16(k_pe)], axis=1).astype(BF16)

    wkv = w_ukv[0].reshape(KV_LORA, MLA_HEADS, QK_NOPE + V_DIM)
    w_ukt = wkv[..., :QK_NOPE].transpose(1, 2, 0)
    w_uvt = wkv[..., QK_NOPE:].transpose(1, 2, 0).astype(BF16)

    wq = w_uq[0].reshape(Q_LORA, MLA_HEADS, QK_NOPE + QK_ROPE)
    wq_n, wq_p = wq[..., :QK_NOPE], wq[..., QK_NOPE:]
    w_qa = _absorb(wq_n.transpose(1, 0, 2), w_ukt)
    w_q = jnp.concatenate(
        [w_qa.transpose(1, 0, 2).reshape(Q_LORA, -1),
         jnp.concatenate([wq_p, _swap16(wq_p)], axis=-1).reshape(Q_LORA, -1).astype(BF16)], axis=1)

    w_pool_b = w_pool[0].astype(BF16)
    w_out_b = w_out_ap[0].astype(BF16)
    npre0, npost0 = norm_pre[0][None], norm_post[0][None]
    npre1, npost1 = norm_pre[1][None], norm_post[1][None]
    qn, kvn, pscale = q_norm[0][None], kv_norm[0][None], pool_scale[0][None]

    w_in_c_b = w_in_c[0].astype(BF16)
    w_s_b = w_s[0].astype(BF16)
    bias = jnp.repeat(b_s[0].T, SGU_GROUP_DIM, axis=1)
    w_out_c_b = w_out_c[0].astype(BF16)
    lng, lnb = sgu_ln_g[0][None], sgu_ln_b[0][None]

    seq_p = x_prompt.shape[1]
    qp, kp, ltp, gap, ppp, ckv_new, kpe_new = _front(
        x_prompt, mod0, row_prompt, npre0, w_in, qn, w_q, kvn,
        w_pool_b, pscale, _no_position_tables(seq_p), tm=seq_p, emit_cache=True)
    xp1 = _attend(qp, kp, ltp, None, w_uvt, gap, ppp, x_prompt, mod0, row_prompt, npost0, w_out_b,
                  tq=seq_p)
    pair = xp1.reshape(xp1.shape[0] // 2, 2 * seq_p, d)
    y_prompt = _gmlp(pair, mod1, row_prompt, npre1, w_in_c_b, lng, lnb, w_s_b, bias, w_out_c_b, npost1,
                     tm=2 * seq_p).reshape(xp1.shape)

    seq_s = x_sample.shape[1]
    kpe_pad = jnp.pad(cache_kpe[:, 0], ((0, 0), (0, 0), (0, QK_ROPE)))
    ctx = _ctx_keys(cache_ckv[:, 0], kpe_pad)
    qs, ks, lts, gas, pps = _front(
        x_sample, mod0, row_sample, npre0, w_in, qn, w_q, kvn,
        w_pool_b, pscale, _rope_tables(seq_s), tm=1024, emit_cache=False)
    xs1 = _attend(qs, ks, lts, ctx, w_uvt, gas, pps, x_sample, mod0, row_sample, npost0, w_out_b,
                  tq=512)
    y_sample = _gmlp(xs1, mod1, row_sample, npre1, w_in_c_b, lng, lnb, w_s_b, bias, w_out_c_b, npost1,
                     tm=512)

    return (y_prompt, y_sample, ckv_new[:, None], kpe_new[:, None])
```

```python
import functools

import jax
import jax.numpy as jnp
import numpy as np
from jax import lax
from jax.experimental import pallas as pl
from jax.experimental.pallas import tpu as pltpu

D_MODEL = 1024
EPS = 1e-6
MLA_HEADS = 4
Q_LORA = 256
KV_LORA = 128
QK_NOPE = 128
QK_ROPE = 64
V_DIM = 128
ROPE_THETA = 10000.0
GRID_W = 64
POOL_WINDOWS = (2, 4, 8, 16)
POOL_GROUP = 128
POOL_WIDTH = len(POOL_WINDOWS) * POOL_GROUP
MLA_WIDTH = MLA_HEADS * V_DIM
CHUNK = 128
SGU_GROUPS = 4
SGU_GROUP_DIM = D_MODEL // SGU_GROUPS
LOG2_E = 1.4426950408889634
Q_SCALE = (QK_NOPE + QK_ROPE) ** -0.5 * LOG2_E

QK_PAD = KV_LORA + 2 * QK_ROPE
BF16_SUBLANES = 16
LAT_ROWS = KV_LORA + BF16_SUBLANES
PROMPT_PACK = 4
KEY_BLOCK = 1024
POOL_HALO = 8
MOD_ROWS = 16
V7X_VMEM_BYTES = 64 * 1024 * 1024
VMEM_LIMIT = V7X_VMEM_BYTES - 8 * 1024 * 1024

_C_QLAT = 0
_C_KVLAT = _C_QLAT + Q_LORA
_C_GATE_A = _C_KVLAT + KV_LORA
_C_POOL = _C_GATE_A + MLA_WIDTH
_C_GATE_B = _C_POOL + POOL_WIDTH
_C_KPE = _C_GATE_B + POOL_WIDTH
_C_END = _C_KPE + 2 * QK_ROPE

BF16 = jnp.bfloat16
F32 = jnp.float32


def _dot(a, b):
    return jnp.dot(a, b, preferred_element_type=F32)


def _dot_nt(a, b):
    return lax.dot_general(a, b, (((1,), (1,)), ((), ())), preferred_element_type=F32)


def _gelu_tanh(x):
    c = np.sqrt(2.0 / np.pi)
    hx = 0.5 * x
    return hx + hx * jnp.tanh(x * (c + (c * 0.044715) * (x * x)))


def _rms(x, g):
    return x * lax.rsqrt(jnp.mean(x * x, axis=-1, keepdims=True) + EPS) * g


def _ada_kernel(c_ref, w_ref, b_ref, o_ref):
    a = jax.nn.silu(c_ref[...]).astype(BF16)
    o_ref[0] = _dot(a, w_ref[0].astype(BF16)) + b_ref[0]


def _ada(cond, w_ada, b_ada):
    depth, d, n = w_ada.shape
    bn = 512
    return pl.pallas_call(
        _ada_kernel,
        grid=(depth, n // bn),
        in_specs=[
            pl.BlockSpec((MOD_ROWS, d), lambda l, j: (0, 0)),
            pl.BlockSpec((1, d, bn), lambda l, j: (l, 0, j)),
            pl.BlockSpec((1, 1, bn), lambda l, j: (l, 0, j)),
        ],
        out_specs=pl.BlockSpec((1, MOD_ROWS, bn), lambda l, j: (l, 0, j)),
        out_shape=jax.ShapeDtypeStruct((depth, MOD_ROWS, n), F32),
        name="ada_mod",
    )(cond, w_ada, b_ada.reshape(depth, 1, n))


def _front_kernel(*refs, tm, seq, halo, emit_cache):
    it = iter(refs)
    x_ref = next(it)
    xp_ref = next(it) if halo else None
    xn_ref = next(it) if halo else None
    mod_ref = next(it)
    npre_ref = next(it)
    win_ref = next(it)
    qn_ref = next(it)
    wq_ref = next(it)
    kvn_ref = next(it)
    wpool_ref = next(it)
    pscale_ref = next(it)
    cos_ref = next(it)
    sin_ref = next(it)
    q_ref = next(it)
    k_ref = next(it)
    lt_ref = next(it)
    ga_ref = next(it)
    pp_ref = next(it)
    ckv_ref = next(it) if emit_cache else None
    kpe_ref = next(it) if emit_cache else None

    i = pl.program_id(0)
    nt = pl.num_programs(0)
    d = D_MODEL
    shift = mod_ref[0, :, 0:d]
    scale = mod_ref[0, :, d:2 * d]
    gain = npre_ref[...] * (1.0 + scale)

    def modulate(xv):
        return (_rms(xv, gain) + shift).astype(BF16)

    def rotate(blk):
        return blk * cos_ref[...] + pltpu.roll(blk, QK_ROPE, axis=1) * sin_ref[...]

    h = modulate(x_ref[0])

    q_lat = _dot(h, win_ref[:, _C_QLAT:_C_KVLAT])
    qn = _rms(q_lat, qn_ref[...]).astype(BF16)
    qa = _dot(qn, wq_ref[:, 0:MLA_HEADS * KV_LORA]) * Q_SCALE
    qp = _dot(qn, wq_ref[:, MLA_HEADS * KV_LORA:]) * Q_SCALE
    for hd in range(MLA_HEADS):
        q_ref[0, hd, :, 0:KV_LORA] = qa[:, hd * KV_LORA:(hd + 1) * KV_LORA].astype(BF16)
        q_ref[0, hd, :, KV_LORA:QK_PAD] = rotate(
            qp[:, hd * 2 * QK_ROPE:(hd + 1) * 2 * QK_ROPE]).astype(BF16)

    ckv = _rms(_dot(h, win_ref[:, _C_KVLAT:_C_GATE_A]), kvn_ref[...])
    kpe2 = _dot(h, win_ref[:, _C_KPE:_C_END])
    if emit_cache:
        ckv_ref[0] = ckv
        kpe_ref[0] = kpe2[:, 0:QK_ROPE]
    k_ref[0, :, 0:KV_LORA] = ckv.astype(BF16)
    k_ref[0, :, KV_LORA:QK_PAD] = rotate(kpe2).astype(BF16)
    lt_ref[0, 0, 0:KV_LORA] = ckv.T.astype(BF16)
    lt_ref[0, 0, KV_LORA:LAT_ROWS] = jnp.ones((BF16_SUBLANES, tm), BF16)

    ga_ref[0] = jax.nn.silu(_dot(h, win_ref[:, _C_GATE_A:_C_POOL])).astype(BF16)

    u = _dot(h, win_ref[:, _C_POOL:_C_GATE_B])
    if halo:
        hh = modulate(jnp.concatenate([xp_ref[0], xn_ref[0]], axis=0))
        uh = _dot(hh, win_ref[:, _C_POOL:_C_GATE_B])
        up = jnp.where(i > 0, uh[0:POOL_HALO], 0.0)
        un = jnp.where(i < nt - 1, uh[POOL_HALO:2 * POOL_HALO], 0.0)
    else:
        up = jnp.zeros((POOL_HALO, POOL_WIDTH), F32)
        un = up
    span = min(tm, seq)
    ext = span + 2 * POOL_HALO
    t = (i * tm) % seq + lax.broadcasted_iota(jnp.int32, (span, 1), 0)
    gate_b = _dot(h, win_ref[:, _C_GATE_B:_C_KPE])
    for g, w in enumerate(POOL_WINDOWS):
        sl = slice(g * POOL_GROUP, (g + 1) * POOL_GROUP)
        cnt = (jnp.minimum(t + w // 2, seq) - jnp.maximum(t - w // 2, 0)).astype(F32)
        deltas = []
        for r0 in range(0, tm, span):
            us = u[r0:r0 + span, sl]
            p = jnp.concatenate([up[:, sl], us, un[:, sl]], axis=0)
            k = 1
            while k < w:
                p = p + pltpu.roll(p, k, axis=0)
                k *= 2
            lead = w // 2 - 1
            if lead:
                p = pltpu.roll(p, ext - lead, axis=0)
            wsum = p[POOL_HALO:POOL_HALO + span]
            deltas.append((wsum * (1.0 / cnt) - us).astype(BF16))
        dlt = deltas[0] if len(deltas) == 1 else jnp.concatenate(deltas, axis=0)
        og = _dot(dlt, wpool_ref[g]) * pscale_ref[:, sl]
        pp_ref[0, :, sl] = (og * jax.nn.silu(gate_b[:, sl])).astype(BF16)


def _front(x, mod, mod_row, npre, w_in, qn, w_q, kvn, w_pool, pscale, tables, *, tm, seq_len,
           emit_cache):
    b, seq, d = x.shape
    assert seq == seq_len or tm == seq
    nt = seq // tm
    halo = nt > 1
    hb = tm // POOL_HALO
    last = seq // POOL_HALO - 1

    def const(shape):
        return pl.BlockSpec(shape, lambda i, j: (0,) * len(shape))

    in_specs = [pl.BlockSpec((1, tm, d), lambda i, j: (j, i, 0))]
    args = [x]
    if halo:
        in_specs += [
            pl.BlockSpec((1, POOL_HALO, d), lambda i, j: (j, jnp.maximum(i * hb - 1, 0), 0)),
            pl.BlockSpec((1, POOL_HALO, d), lambda i, j: (j, jnp.minimum((i + 1) * hb, last), 0)),
        ]
        args += [x, x]
    in_specs += [
        pl.BlockSpec((1, 1, 3 * d), lambda i, j: (mod_row(j), 0, 0)),
        const((1, d)), const(w_in.shape), const((1, Q_LORA)), const(w_q.shape),
        const((1, KV_LORA)), const(w_pool.shape), const((1, POOL_WIDTH)),
        pl.BlockSpec((tm, 2 * QK_ROPE), lambda i, j: (i, 0)),
        pl.BlockSpec((tm, 2 * QK_ROPE), lambda i, j: (i, 0)),
    ]
    args += [mod, npre, w_in, qn, w_q, kvn, w_pool, pscale, *tables]

    out_specs = [
        pl.BlockSpec((1, MLA_HEADS, tm, QK_PAD), lambda i, j: (j, 0, i, 0)),
        pl.BlockSpec((1, tm, QK_PAD), lambda i, j: (j, i, 0)),
        pl.BlockSpec((1, 1, LAT_ROWS, tm), lambda i, j: (j, i, 0, 0)),
        pl.BlockSpec((1, tm, MLA_WIDTH), lambda i, j: (j, i, 0)),
        pl.BlockSpec((1, tm, POOL_WIDTH), lambda i, j: (j, i, 0)),
    ]
    out_shape = [
        jax.ShapeDtypeStruct((b, MLA_HEADS, seq, QK_PAD), BF16),
        jax.ShapeDtypeStruct((b, seq, QK_PAD), BF16),
        jax.ShapeDtypeStruct((b, nt, LAT_ROWS, tm), BF16),
        jax.ShapeDtypeStruct((b, seq, MLA_WIDTH), BF16),
        jax.ShapeDtypeStruct((b, seq, POOL_WIDTH), BF16),
    ]
    if emit_cache:
        out_specs += [
            pl.BlockSpec((1, tm, KV_LORA), lambda i, j: (j, i, 0)),
            pl.BlockSpec((1, tm, QK_ROPE), lambda i, j: (j, i, 0)),
        ]
        out_shape += [
            jax.ShapeDtypeStruct((b, seq, KV_LORA), F32),
            jax.ShapeDtypeStruct((b, seq, QK_ROPE), F32),
        ]
    return pl.pallas_call(
        functools.partial(_front_kernel, tm=tm, seq=seq_len, halo=halo, emit_cache=emit_cache),
        grid=(nt, b),
        in_specs=in_specs,
        out_specs=out_specs,
        out_shape=out_shape,
        compiler_params=pltpu.CompilerParams(
            dimension_semantics=("arbitrary", "arbitrary"), vmem_limit_bytes=VMEM_LIMIT),
        name="mla_pool_front",
    )(*args)


def _ctx_kernel(ckv_ref, kpe_ref, k_ref, lt_ref):
    ckv = ckv_ref[0]
    k_ref[0, :, 0:KV_LORA] = ckv.astype(BF16)
    k_ref[0, :, KV_LORA:QK_PAD] = kpe_ref[0].astype(BF16)
    lt_ref[0, 0:KV_LORA] = ckv.T.astype(BF16)
    lt_ref[0, KV_LORA:LAT_ROWS] = jnp.ones((BF16_SUBLANES, ckv.shape[0]), BF16)


def _ctx_keys(ckv, kpe_pad):
    b, past, _ = ckv.shape
    return pl.pallas_call(
        _ctx_kernel,
        grid=(b,),
        in_specs=[
            pl.BlockSpec((1, past, KV_LORA), lambda j: (j, 0, 0)),
            pl.BlockSpec((1, past, 2 * QK_ROPE), lambda j: (j, 0, 0)),
        ],
        out_specs=[
            pl.BlockSpec((1, past, QK_PAD), lambda j: (j, 0, 0)),
            pl.BlockSpec((1, LAT_ROWS, past), lambda j: (j, 0, 0)),
        ],
        out_shape=[
            jax.ShapeDtypeStruct((b, past, QK_PAD), BF16),
            jax.ShapeDtypeStruct((b, LAT_ROWS, past), BF16),
        ],
        name="ctx_keys",
    )(ckv, kpe_pad)


def _key_blocks(k_ref, lt_ref, kc_ref, ltc_ref):
    nchunks, kc = lt_ref.shape[1], lt_ref.shape[3]
    kb = min(KEY_BLOCK, kc)
    blocks = []
    for c in range(nchunks):
        for r in range(0, kc, kb):
            blocks.append((kb, k_ref.at[0, c * kc + r:c * kc + r + kb, :],
                           lt_ref.at[0, c, :, r:r + kb]))
    if kc_ref is not None:
        blocks.append((kc_ref.shape[1], kc_ref.at[0], ltc_ref.at[0]))
    return blocks


def _attention_tile(q_ref, blocks, wuvt_ref, ga_ref, pp_ref, wout_ref, mix_scr, s_bufs):
    tq = q_ref.shape[2]
    sub = 8

    zero = jnp.minimum(pl.program_id(0), 0)
    starts = []
    r0 = 0
    for n, _, _ in blocks:
        starts.append(r0)
        r0 += n

    def rows(bi):
        return pl.ds(pl.multiple_of(starts[bi] + zero, sub), blocks[bi][0])

    def score_head(hd):
        m8 = None
        for bi, (n, kblk, _) in enumerate(blocks):
            s = _dot_nt(kblk[...], q_ref[0, hd])
            s_bufs[hd % 2][rows(bi), :] = s
            cm = jnp.max(s.reshape(n // sub, sub, tq), axis=0)
            m8 = cm if m8 is None else jnp.maximum(m8, cm)
        return m8

    m8 = score_head(0)
    for hd in range(MLA_HEADS):
        m = jnp.max(m8, axis=0, keepdims=True)
        if hd + 1 < MLA_HEADS:
            m8 = score_head(hd + 1)
        else:
            out = _dot(pp_ref[0], wout_ref[MLA_WIDTH:, :])
        acc = None
        for bi, (_, _, ltblk) in enumerate(blocks):
            p = jnp.exp2(s_bufs[hd % 2][rows(bi), :] - m).astype(BF16)
            part = _dot(ltblk[...], p)
            acc = part if acc is None else acc + part
        o_lat = (acc[0:KV_LORA] * (1.0 / acc[KV_LORA:KV_LORA + 1])).astype(BF16)
        o = _dot(wuvt_ref[hd], o_lat).T
        sl = slice(hd * V_DIM, (hd + 1) * V_DIM)
        mix_scr[:, sl] = (o * ga_ref[0, :, sl].astype(F32)).astype(BF16)
    return out + _dot(mix_scr[...], wout_ref[0:MLA_WIDTH, :])


def _attn_kernel(*refs, has_ctx):
    it = iter(refs)
    q_ref = next(it)
    k_ref = next(it)
    lt_ref = next(it)
    kc_ref = next(it) if has_ctx else None
    ltc_ref = next(it) if has_ctx else None
    wuvt_ref = next(it)
    ga_ref = next(it)
    pp_ref = next(it)
    x_ref = next(it)
    mod_ref = next(it)
    npost_ref = next(it)
    wout_ref = next(it)
    o_ref = next(it)
    mix_scr = next(it)
    s_bufs = (next(it), next(it))

    blocks = _key_blocks(k_ref, lt_ref, kc_ref, ltc_ref)
    out = _attention_tile(q_ref, blocks, wuvt_ref, ga_ref, pp_ref, wout_ref, mix_scr, s_bufs)
    gate = mod_ref[0, :, 2 * D_MODEL:3 * D_MODEL]
    o_ref[0] = x_ref[0] + _rms(out, npost_ref[...] * gate)


def _attend(q, k, lt, ctx, w_uvt, ga, pp, x, mod, mod_row, npost, w_out, *, tq, pack):
    b, seq, d = x.shape
    has_ctx = ctx is not None
    n_keys = seq + (ctx[0].shape[1] if has_ctx else 0)
    nq = seq // tq

    def const(shape):
        return pl.BlockSpec(shape, lambda j, i: (0,) * len(shape))

    def tiled(width):
        return pl.BlockSpec((1, tq, width), lambda j, i: (j // pack, (j % pack) * nq + i, 0))

    if pack == 1:
        nchunks, kc = lt.shape[1], lt.shape[3]
        lt_spec = pl.BlockSpec((1, nchunks, LAT_ROWS, kc), lambda j, i: (j, 0, 0, 0))
    else:
        assert lt.shape[1] == 1
        lt_spec = pl.BlockSpec((1, 1, LAT_ROWS, seq), lambda j, i: (j // pack, 0, 0, j % pack))
    in_specs = [
        pl.BlockSpec((1, MLA_HEADS, tq, QK_PAD), lambda j, i: (j // pack, 0, (j % pack) * nq + i, 0)),
        pl.BlockSpec((1, seq, QK_PAD), lambda j, i: (j // pack, j % pack, 0)),
        lt_spec,
    ]
    args = [q, k, lt]
    if has_ctx:
        kctx, ltctx = ctx
        past = kctx.shape[1]
        in_specs += [
            pl.BlockSpec((1, past, QK_PAD), lambda j, i: (j, 0, 0)),
            pl.BlockSpec((1, LAT_ROWS, past), lambda j, i: (j, 0, 0)),
        ]
        args += [kctx, ltctx]
    in_specs += [
        const(w_uvt.shape),
        tiled(MLA_WIDTH),
        tiled(POOL_WIDTH),
        pl.BlockSpec((1, tq, d), lambda j, i: (j, i, 0)),
        pl.BlockSpec((1, 1, 3 * d), lambda j, i: (mod_row(j), 0, 0)),
        const((1, d)), const(w_out.shape),
    ]
    args += [w_uvt, ga, pp, x, mod, npost, w_out]
    return pl.pallas_call(
        functools.partial(_attn_kernel, has_ctx=has_ctx),
        grid=(b, nq),
        in_specs=in_specs,
        out_specs=pl.BlockSpec((1, tq, d), lambda j, i: (j, i, 0)),
        out_shape=jax.ShapeDtypeStruct((b, seq, d), F32),
        scratch_shapes=[pltpu.VMEM((tq, MLA_WIDTH), BF16),
                        pltpu.VMEM((n_keys, tq), F32), pltpu.VMEM((n_keys, tq), F32)],
        compiler_params=pltpu.CompilerParams(
            dimension_semantics=("arbitrary", "arbitrary"), vmem_limit_bytes=VMEM_LIMIT),
        name="mla_attend_out",
    )(*args)


def _gmlp_stages(load_x, store_y, mod_ref, npre_ref, win_ref, lng_ref, lnb_ref, ws_ref, bs_ref,
                 wout_ref, npost_ref, z_scr):
    d = D_MODEL
    tm = z_scr.shape[0]
    st = {}

    def project_u():
        shift = mod_ref[0, :, 0:d]
        scale = mod_ref[0, :, d:2 * d]
        st["h"] = (_rms(load_x(), npre_ref[...] * (1.0 + scale)) + shift).astype(BF16)
        st["u"] = _dot(st["h"], win_ref[:, 0:d])

    def project_v():
        st["v"] = _dot(st["h"], win_ref[:, d:2 * d])

    def project_gate():
        st["g"] = _dot(st.pop("h"), win_ref[:, 2 * d:3 * d])

    def mix_and_project():
        v = _gelu_tanh(st.pop("v"))
        mu = jnp.mean(v, axis=-1, keepdims=True)
        vc = v - mu
        var = jnp.mean(vc * vc, axis=-1, keepdims=True)
        vn = (vc * lax.rsqrt(var + EPS) * lng_ref[...] + lnb_ref[...]).astype(BF16)
        us = _gelu_tanh(st.pop("u")) * jax.nn.silu(st.pop("g"))
        for n in range(tm // CHUNK):
            rows = slice(n * CHUNK, (n + 1) * CHUNK)
            for g in range(SGU_GROUPS):
                cols = slice(g * SGU_GROUP_DIM, (g + 1) * SGU_GROUP_DIM)
                sv = _dot(ws_ref[g], vn[rows, cols]) + bs_ref[:, cols]
                z_scr[rows, cols] = (us[rows, cols] * sv).astype(BF16)
        out = _dot(z_scr[...], wout_ref[...])
        gate = mod_ref[0, :, 2 * d:3 * d]
        store_y(load_x() + _rms(out, npost_ref[...] * gate))

    return [project_u, project_v, project_gate, mix_and_project]


def _gmlp_kernel(x_ref, mod_ref, npre_ref, win_ref, lng_ref, lnb_ref, ws_ref, bs_ref, wout_ref,
                 npost_ref, o_ref, z_scr):
    def store_y(y):
        o_ref[0] = y

    for stage in _gmlp_stages(lambda: x_ref[0], store_y, mod_ref, npre_ref, win_ref, lng_ref, lnb_ref,
                              ws_ref, bs_ref, wout_ref, npost_ref, z_scr):
        stage()


def _gmlp(x, mod, mod_row, npre, w_in, lng, lnb, w_s, bias, w_out, npost, *, tm):
    b, seq, d = x.shape

    def const(shape):
        return pl.BlockSpec(shape, lambda j, i: (0,) * len(shape))

    return pl.pallas_call(
        _gmlp_kernel,
        grid=(b, seq // tm),
        in_specs=[
            pl.BlockSpec((1, tm, d), lambda j, i: (j, i, 0)),
            pl.BlockSpec((1, 1, 3 * d), lambda j, i: (mod_row(j), 0, 0)),
            const((1, d)), const(w_in.shape), const((1, d)), const((1, d)),
            const(w_s.shape), const(bias.shape), const(w_out.shape), const((1, d)),
        ],
        out_specs=pl.BlockSpec((1, tm, d), lambda j, i: (j, i, 0)),
        out_shape=jax.ShapeDtypeStruct((b, seq, d), F32),
        scratch_shapes=[pltpu.VMEM((tm, d), BF16)],
        compiler_params=pltpu.CompilerParams(
            dimension_semantics=("arbitrary", "arbitrary"), vmem_limit_bytes=VMEM_LIMIT),
        name="gmlp_layer",
    )(x, mod, npre, w_in, lng, lnb, w_s, bias, w_out, npost)


def _swap16(w):
    half = QK_ROPE // 4
    parts = [w[..., k * half:(k + 1) * half] for k in range(4)]
    return jnp.concatenate([parts[1], parts[0], parts[3], parts[2]], axis=-1)


def _rope_tables(seq):
    t = np.arange(seq)
    half = QK_ROPE // 4
    inv = ROPE_THETA ** (-np.arange(half, dtype=np.float64) / half)
    ang_r = (t // GRID_W)[:, None] * inv
    ang_c = (t % GRID_W)[:, None] * inv
    cr, sr, cc, sc = np.cos(ang_r), np.sin(ang_r), np.cos(ang_c), np.sin(ang_c)
    zero = np.zeros((seq, QK_ROPE))
    cos = np.concatenate([cr, cr, cc, cc, zero], axis=-1)
    sin = np.concatenate([-sr, sr, -sc, sc, zero], axis=-1)
    return jnp.asarray(cos, F32), jnp.asarray(sin, F32)


def _no_position_tables(seq):
    one = np.concatenate([np.ones((seq, QK_ROPE)), np.zeros((seq, QK_ROPE))], axis=-1)
    return jnp.asarray(one, F32), jnp.zeros((seq, 2 * QK_ROPE), F32)


def _absorb_kernel(wqn_ref, wukt_ref, o_ref):
    for hd in range(MLA_HEADS):
        o_ref[hd] = _dot(wqn_ref[hd].astype(BF16), wukt_ref[hd].astype(BF16)).astype(BF16)


def _absorb(wq_nope, w_ukt):
    return pl.pallas_call(
        _absorb_kernel,
        out_shape=jax.ShapeDtypeStruct((MLA_HEADS, Q_LORA, KV_LORA), BF16),
        name="absorb_q",
    )(wq_nope, w_ukt)


def kernel(x_prompt, x_sample, cache_ckv, cache_kpe, c, c_ctx, w_ada, b_ada, norm_pre, norm_post,
           w_in_ap, q_norm, w_uq, kv_norm, w_ukv, w_pool, pool_scale, w_out_ap,
           w_in_c, sgu_ln_g, sgu_ln_b, w_s, b_s, w_out_c):
    d = D_MODEL
    dec_b = x_sample.shape[0]
    ctx_row = dec_b

    cond = jnp.zeros((MOD_ROWS, d), F32).at[:dec_b].set(c).at[ctx_row].set(c_ctx)
    mod = _ada(cond, w_ada, b_ada)
    mod0 = mod[0].reshape(MOD_ROWS, 1, 3 * d)
    mod1 = mod[1].reshape(MOD_ROWS, 1, 3 * d)
    row_sample = lambda j: j
    row_prompt = lambda j: ctx_row

    wi = w_in_ap[0]
    q_lat, kv_lat, k_pe, gate_a, pool_in, gate_b = jnp.split(
        wi, (Q_LORA, Q_LORA + KV_LORA, Q_LORA + KV_LORA + QK_ROPE,
             Q_LORA + KV_LORA + QK_ROPE + MLA_WIDTH,
             Q_LORA + KV_LORA + QK_ROPE + MLA_WIDTH + POOL_WIDTH), axis=1)
    w_in = jnp.concatenate(
        [q_lat, kv_lat, gate_a, pool_in, gate_b, k_pe, _swap16(k_pe)], axis=1).astype(BF16)

    wkv = w_ukv[0].reshape(KV_LORA, MLA_HEADS, QK_NOPE + V_DIM)
    w_ukt = wkv[..., :QK_NOPE].transpose(1, 2, 0)
    w_uvt = wkv[..., QK_NOPE:].transpose(1, 2, 0).astype(BF16)

    wq = w_uq[0].reshape(Q_LORA, MLA_HEADS, QK_NOPE + QK_ROPE)
    wq_n, wq_p = wq[..., :QK_NOPE], wq[..., QK_NOPE:]
    w_qa = _absorb(wq_n.transpose(1, 0, 2), w_ukt)
    w_q = jnp.concatenate(
        [w_qa.transpose(1, 0, 2).reshape(Q_LORA, -1),
         jnp.concatenate([wq_p, _swap16(wq_p)], axis=-1).reshape(Q_LORA, -1).astype(BF16)], axis=1)

    w_pool_b = w_pool[0].astype(BF16)
    w_out_b = w_out_ap[0].astype(BF16)
    npre0, npost0 = norm_pre[0][None], norm_post[0][None]
    npre1, npost1 = norm_pre[1][None], norm_post[1][None]
    qn, kvn, pscale = q_norm[0][None], kv_norm[0][None], pool_scale[0][None]

    w_in_c_b = w_in_c[0].astype(BF16)
    w_s_b = w_s[0].astype(BF16)
    bias = jnp.repeat(b_s[0].T, SGU_GROUP_DIM, axis=1)
    w_out_c_b = w_out_c[0].astype(BF16)
    lng, lnb = sgu_ln_g[0][None], sgu_ln_b[0][None]

    n_p, seq_p = x_prompt.shape[0], x_prompt.shape[1]
    packed = x_prompt.reshape(n_p // PROMPT_PACK, PROMPT_PACK * seq_p, d)
    qp, kp, ltp, gap, ppp, ckv_new, kpe_new = _front(
        packed, mod0, row_prompt, npre0, w_in, qn, w_q, kvn, w_pool_b, pscale,
        _no_position_tables(PROMPT_PACK * seq_p), tm=PROMPT_PACK * seq_p, seq_len=seq_p, emit_cache=True)
    ckv_new = ckv_new.reshape(n_p, seq_p, KV_LORA)
    kpe_new = kpe_new.reshape(n_p, seq_p, QK_ROPE)
    xp1 = _attend(qp, kp, ltp, None, w_uvt, gap, ppp, x_prompt, mod0, row_prompt, npost0, w_out_b,
                  tq=seq_p, pack=PROMPT_PACK)
    pair = xp1.reshape(xp1.shape[0] // 2, 2 * seq_p, d)
    y_prompt = _gmlp(pair, mod1, row_prompt, npre1, w_in_c_b, lng, lnb, w_s_b, bias, w_out_c_b, npost1,
                     tm=2 * seq_p).reshape(xp1.shape)

    seq_s = x_sample.shape[1]
    kpe_pad = jnp.pad(cache_kpe[:, 0], ((0, 0), (0, 0), (0, QK_ROPE)))
    ctx = _ctx_keys(cache_ckv[:, 0], kpe_pad)
    qs, ks, lts, gas, pps = _front(
        x_sample, mod0, row_sample, npre0, w_in, qn, w_q, kvn,
        w_pool_b, pscale, _rope_tables(seq_s), tm=1024, seq_len=seq_s, emit_cache=False)
    xs1 = _attend(qs, ks, lts, ctx, w_uvt, gas, pps, x_sample, mod0, row_sample, npost0, w_out_b,
                  tq=512, pack=1)
    y_sample = _gmlp(xs1, mod1, row_sample, npre1, w_in_c_b, lng, lnb, w_s_b, bias, w_out_c_b, npost1,
                     tm=512)

    return (y_prompt, y_sample, ckv_new[:, None], kpe_new[:, None])
```

```python
import functools

import jax
import jax.numpy as jnp
import numpy as np
from jax import lax
from jax.experimental import pallas as pl
from jax.experimental.pallas import tpu as pltpu

D_MODEL = 1024
EPS = 1e-6
MLA_HEADS = 4
Q_LORA = 256
KV_LORA = 128
QK_NOPE = 128
QK_ROPE = 64
V_DIM = 128
ROPE_THETA = 10000.0
GRID_W = 64
POOL_WINDOWS = (2, 4, 8, 16)
POOL_GROUP = 128
POOL_WIDTH = len(POOL_WINDOWS) * POOL_GROUP
MLA_WIDTH = MLA_HEADS * V_DIM
CHUNK = 128
SGU_GROUPS = 4
SGU_GROUP_DIM = D_MODEL // SGU_GROUPS
LOG2_E = 1.4426950408889634
Q_SCALE = (QK_NOPE + QK_ROPE) ** -0.5 * LOG2_E

QK_PAD = KV_LORA + 2 * QK_ROPE
BF16_SUBLANES = 16
LAT_ROWS = KV_LORA + BF16_SUBLANES
FRONT_ROW_GROUPS = 4
PROMPT_PACK = 4
KEY_BLOCK = 1024
POOL_HALO = 8
MOD_ROWS = 16
V7X_VMEM_BYTES = 64 * 1024 * 1024
VMEM_LIMIT = V7X_VMEM_BYTES - 8 * 1024 * 1024

_C_QLAT = 0
_C_KVLAT = _C_QLAT + Q_LORA
_C_GATE_A = _C_KVLAT + KV_LORA
_C_POOL = _C_GATE_A + MLA_WIDTH
_C_GATE_B = _C_POOL + POOL_WIDTH
_C_KPE = _C_GATE_B + POOL_WIDTH
_C_END = _C_KPE + 2 * QK_ROPE

BF16 = jnp.bfloat16
F32 = jnp.float32


def _dot(a, b):
    return jnp.dot(a, b, preferred_element_type=F32)


def _dot_nt(a, b):
    return lax.dot_general(a, b, (((1,), (1,)), ((), ())), preferred_element_type=F32)


def _gelu_tanh(x):
    c = np.sqrt(2.0 / np.pi)
    hx = 0.5 * x
    return hx + hx * jnp.tanh(x * (c + (c * 0.044715) * (x * x)))


def _rms(x, g):
    return x * lax.rsqrt(jnp.mean(x * x, axis=-1, keepdims=True) + EPS) * g


def _ada_kernel(c_ref, w_ref, b_ref, o_ref):
    a = jax.nn.silu(c_ref[...]).astype(BF16)
    o_ref[0] = _dot(a, w_ref[0].astype(BF16)) + b_ref[0]


def _ada(cond, w_ada, b_ada):
    depth, d, n = w_ada.shape
    bn = 512
    return pl.pallas_call(
        _ada_kernel,
        grid=(depth, n // bn),
        in_specs=[
            pl.BlockSpec((MOD_ROWS, d), lambda l, j: (0, 0)),
            pl.BlockSpec((1, d, bn), lambda l, j: (l, 0, j)),
            pl.BlockSpec((1, 1, bn), lambda l, j: (l, 0, j)),
        ],
        out_specs=pl.BlockSpec((1, MOD_ROWS, bn), lambda l, j: (l, 0, j)),
        out_shape=jax.ShapeDtypeStruct((depth, MOD_ROWS, n), F32),
        name="ada_mod",
    )(cond, w_ada, b_ada.reshape(depth, 1, n))


def _front_kernel(*refs, tm, seq, halo, emit_cache):
    it = iter(refs)
    x_ref = next(it)
    xp_ref = next(it) if halo else None
    xn_ref = next(it) if halo else None
    mod_ref = next(it)
    npre_ref = next(it)
    win_ref = next(it)
    qn_ref = next(it)
    wq_ref = next(it)
    kvn_ref = next(it)
    wpool_ref = next(it)
    pscale_ref = next(it)
    cos_ref = next(it)
    sin_ref = next(it)
    q_ref = next(it)
    k_ref = next(it)
    lt_ref = next(it)
    ga_ref = next(it)
    pp_ref = next(it)
    ckv_ref = next(it) if emit_cache else None
    kpe_ref = next(it) if emit_cache else None

    i = pl.program_id(0)
    nt = pl.num_programs(0)
    d = D_MODEL
    shift = mod_ref[0, :, 0:d]
    scale = mod_ref[0, :, d:2 * d]
    gain = npre_ref[...] * (1.0 + scale)

    def modulate(xv):
        return (_rms(xv, gain) + shift).astype(BF16)

    def rotate(blk):
        return blk * cos_ref[...] + pltpu.roll(blk, QK_ROPE, axis=1) * sin_ref[...]

    rg = tm // FRONT_ROW_GROUPS
    h_parts = [modulate(x_ref[0, r:r + rg]) for r in range(0, tm, rg)]

    def project(c0, c1):
        return jnp.concatenate([_dot(hp, win_ref[:, c0:c1]) for hp in h_parts], axis=0)

    q_lat = project(_C_QLAT, _C_KVLAT)
    qn = _rms(q_lat, qn_ref[...]).astype(BF16)
    qa = _dot(qn, wq_ref[:, 0:MLA_HEADS * KV_LORA]) * Q_SCALE
    qp = _dot(qn, wq_ref[:, MLA_HEADS * KV_LORA:]) * Q_SCALE
    for hd in range(MLA_HEADS):
        q_ref[0, hd, :, 0:KV_LORA] = qa[:, hd * KV_LORA:(hd + 1) * KV_LORA].astype(BF16)
        q_ref[0, hd, :, KV_LORA:QK_PAD] = rotate(
            qp[:, hd * 2 * QK_ROPE:(hd + 1) * 2 * QK_ROPE]).astype(BF16)

    ckv = _rms(project(_C_KVLAT, _C_GATE_A), kvn_ref[...])
    kpe2 = project(_C_KPE, _C_END)
    if emit_cache:
        ckv_ref[0] = ckv
        kpe_ref[0] = kpe2[:, 0:QK_ROPE]
    k_ref[0, :, 0:KV_LORA] = ckv.astype(BF16)
    k_ref[0, :, KV_LORA:QK_PAD] = rotate(kpe2).astype(BF16)
    lt_ref[0, 0, 0:KV_LORA] = ckv.T.astype(BF16)
    lt_ref[0, 0, KV_LORA:LAT_ROWS] = jnp.ones((BF16_SUBLANES, tm), BF16)

    ga_ref[0] = jax.nn.silu(project(_C_GATE_A, _C_POOL)).astype(BF16)

    u = project(_C_POOL, _C_GATE_B)
    if halo:
        hh = modulate(jnp.concatenate([xp_ref[0], xn_ref[0]], axis=0))
        uh = _dot(hh, win_ref[:, _C_POOL:_C_GATE_B])
        up = jnp.where(i > 0, uh[0:POOL_HALO], 0.0)
        un = jnp.where(i < nt - 1, uh[POOL_HALO:2 * POOL_HALO], 0.0)
    else:
        up = jnp.zeros((POOL_HALO, POOL_WIDTH), F32)
        un = up
    span = min(tm, seq)
    ext = span + 2 * POOL_HALO
    t = (i * tm) % seq + lax.broadcasted_iota(jnp.int32, (span, 1), 0)
    gate_b = project(_C_GATE_B, _C_KPE)
    for g, w in enumerate(POOL_WINDOWS):
        sl = slice(g * POOL_GROUP, (g + 1) * POOL_GROUP)
        cnt = (jnp.minimum(t + w // 2, seq) - jnp.maximum(t - w // 2, 0)).astype(F32)
        deltas = []
        for r0 in range(0, tm, span):
            us = u[r0:r0 + span, sl]
            p = jnp.concatenate([up[:, sl], us, un[:, sl]], axis=0)
            k = 1
            while k < w:
                p = p + pltpu.roll(p, k, axis=0)
                k *= 2
            lead = w // 2 - 1
            if lead:
                p = pltpu.roll(p, ext - lead, axis=0)
            wsum = p[POOL_HALO:POOL_HALO + span]
            deltas.append((wsum * (1.0 / cnt) - us).astype(BF16))
        dlt = deltas[0] if len(deltas) == 1 else jnp.concatenate(deltas, axis=0)
        og = _dot(dlt, wpool_ref[g]) * pscale_ref[:, sl]
        pp_ref[0, :, sl] = (og * jax.nn.silu(gate_b[:, sl])).astype(BF16)


def _front(x, mod, mod_row, npre, w_in, qn, w_q, kvn, w_pool, pscale, tables, *, tm, seq_len,
           emit_cache):
    b, seq, d = x.shape
    assert seq == seq_len or tm == seq
    nt = seq // tm
    halo = nt > 1
    hb = tm // POOL_HALO
    last = seq // POOL_HALO - 1

    def const(shape):
        return pl.BlockSpec(shape, lambda i, j: (0,) * len(shape))

    in_specs = [pl.BlockSpec((1, tm, d), lambda i, j: (j, i, 0))]
    args = [x]
    if halo:
        in_specs += [
            pl.BlockSpec((1, POOL_HALO, d), lambda i, j: (j, jnp.maximum(i * hb - 1, 0), 0)),
            pl.BlockSpec((1, POOL_HALO, d), lambda i, j: (j, jnp.minimum((i + 1) * hb, last), 0)),
        ]
        args += [x, x]
    in_specs += [
        pl.BlockSpec((1, 1, 3 * d), lambda i, j: (mod_row(j), 0, 0)),
        const((1, d)), const(w_in.shape), const((1, Q_LORA)), const(w_q.shape),
        const((1, KV_LORA)), const(w_pool.shape), const((1, POOL_WIDTH)),
        pl.BlockSpec((tm, 2 * QK_ROPE), lambda i, j: (i, 0)),
        pl.BlockSpec((tm, 2 * QK_ROPE), lambda i, j: (i, 0)),
    ]
    args += [mod, npre, w_in, qn, w_q, kvn, w_pool, pscale, *tables]

    out_specs = [
        pl.BlockSpec((1, MLA_HEADS, tm, QK_PAD), lambda i, j: (j, 0, i, 0)),
        pl.BlockSpec((1, tm, QK_PAD), lambda i, j: (j, i, 0)),
        pl.BlockSpec((1, 1, LAT_ROWS, tm), lambda i, j: (j, i, 0, 0)),
        pl.BlockSpec((1, tm, MLA_WIDTH), lambda i, j: (j, i, 0)),
        pl.BlockSpec((1, tm, POOL_WIDTH), lambda i, j: (j, i, 0)),
    ]
    out_shape = [
        jax.ShapeDtypeStruct((b, MLA_HEADS, seq, QK_PAD), BF16),
        jax.ShapeDtypeStruct((b, seq, QK_PAD), BF16),
        jax.ShapeDtypeStruct((b, nt, LAT_ROWS, tm), BF16),
        jax.ShapeDtypeStruct((b, seq, MLA_WIDTH), BF16),
        jax.ShapeDtypeStruct((b, seq, POOL_WIDTH), BF16),
    ]
    if emit_cache:
        out_specs += [
            pl.BlockSpec((1, tm, KV_LORA), lambda i, j: (j, i, 0)),
            pl.BlockSpec((1, tm, QK_ROPE), lambda i, j: (j, i, 0)),
        ]
        out_shape += [
            jax.ShapeDtypeStruct((b, seq, KV_LORA), F32),
            jax.ShapeDtypeStruct((b, seq, QK_ROPE), F32),
        ]
    return pl.pallas_call(
        functools.partial(_front_kernel, tm=tm, seq=seq_len, halo=halo, emit_cache=emit_cache),
        grid=(nt, b),
        in_specs=in_specs,
        out_specs=out_specs,
        out_shape=out_shape,
        compiler_params=pltpu.CompilerParams(
            dimension_semantics=("arbitrary", "arbitrary"), vmem_limit_bytes=VMEM_LIMIT),
        name="mla_pool_front",
    )(*args)


def _ctx_kernel(ckv_ref, kpe_ref, k_ref, lt_ref):
    ckv = ckv_ref[0]
    k_ref[0, :, 0:KV_LORA] = ckv.astype(BF16)
    k_ref[0, :, KV_LORA:QK_PAD] = kpe_ref[0].astype(BF16)
    lt_ref[0, 0:KV_LORA] = ckv.T.astype(BF16)
    lt_ref[0, KV_LORA:LAT_ROWS] = jnp.ones((BF16_SUBLANES, ckv.shape[0]), BF16)


def _ctx_keys(ckv, kpe_pad):
    b, past, _ = ckv.shape
    return pl.pallas_call(
        _ctx_kernel,
        grid=(b,),
        in_specs=[
            pl.BlockSpec((1, past, KV_LORA), lambda j: (j, 0, 0)),
            pl.BlockSpec((1, past, 2 * QK_ROPE), lambda j: (j, 0, 0)),
        ],
        out_specs=[
            pl.BlockSpec((1, past, QK_PAD), lambda j: (j, 0, 0)),
            pl.BlockSpec((1, LAT_ROWS, past), lambda j: (j, 0, 0)),
        ],
        out_shape=[
            jax.ShapeDtypeStruct((b, past, QK_PAD), BF16),
            jax.ShapeDtypeStruct((b, LAT_ROWS, past), BF16),
        ],
        name="ctx_keys",
    )(ckv, kpe_pad)


def _key_blocks(k_ref, lt_ref, kc_ref, ltc_ref):
    nchunks, kc = lt_ref.shape[1], lt_ref.shape[3]
    kb = min(KEY_BLOCK, kc)
    blocks = []
    for c in range(nchunks):
        for r in range(0, kc, kb):
            blocks.append((kb, k_ref.at[0, c * kc + r:c * kc + r + kb, :],
                           lt_ref.at[0, c, :, r:r + kb]))
    if kc_ref is not None:
        blocks.append((kc_ref.shape[1], kc_ref.at[0], ltc_ref.at[0]))
    return blocks


def _attention_tile(q_ref, blocks, wuvt_ref, ga_ref, pp_ref, wout_ref, mix_scr, s_bufs):
    tq = q_ref.shape[2]
    sub = 8

    zero = jnp.minimum(pl.program_id(0), 0)
    starts = []
    r0 = 0
    for n, _, _ in blocks:
        starts.append(r0)
        r0 += n

    def rows(bi):
        return pl.ds(pl.multiple_of(starts[bi] + zero, sub), blocks[bi][0])

    def score_head(hd):
        m8 = None
        for bi, (n, kblk, _) in enumerate(blocks):
            s = _dot_nt(kblk[...], q_ref[0, hd])
            s_bufs[hd % 2][rows(bi), :] = s
            cm = jnp.max(s.reshape(n // sub, sub, tq), axis=0)
            m8 = cm if m8 is None else jnp.maximum(m8, cm)
        return m8

    m8 = score_head(0)
    for hd in range(MLA_HEADS):
        m = jnp.max(m8, axis=0, keepdims=True)
        if hd + 1 < MLA_HEADS:
            m8 = score_head(hd + 1)
        else:
            out = _dot(pp_ref[0], wout_ref[MLA_WIDTH:, :])
        acc = None
        for bi, (_, _, ltblk) in enumerate(blocks):
            p = jnp.exp2(s_bufs[hd % 2][rows(bi), :] - m).astype(BF16)
            part = _dot(ltblk[...], p)
            acc = part if acc is None else acc + part
        o_lat = (acc[0:KV_LORA] * (1.0 / acc[KV_LORA:KV_LORA + 1])).astype(BF16)
        o = _dot(wuvt_ref[hd], o_lat).T
        sl = slice(hd * V_DIM, (hd + 1) * V_DIM)
        mix_scr[:, sl] = (o * ga_ref[0, :, sl].astype(F32)).astype(BF16)
    return out + _dot(mix_scr[...], wout_ref[0:MLA_WIDTH, :])


def _attn_kernel(*refs, has_ctx):
    it = iter(refs)
    q_ref = next(it)
    k_ref = next(it)
    lt_ref = next(it)
    kc_ref = next(it) if has_ctx else None
    ltc_ref = next(it) if has_ctx else None
    wuvt_ref = next(it)
    ga_ref = next(it)
    pp_ref = next(it)
    x_ref = next(it)
    mod_ref = next(it)
    npost_ref = next(it)
    wout_ref = next(it)
    o_ref = next(it)
    mix_scr = next(it)
    s_bufs = (next(it), next(it))

    blocks = _key_blocks(k_ref, lt_ref, kc_ref, ltc_ref)
    out = _attention_tile(q_ref, blocks, wuvt_ref, ga_ref, pp_ref, wout_ref, mix_scr, s_bufs)
    gate = mod_ref[0, :, 2 * D_MODEL:3 * D_MODEL]
    o_ref[0] = x_ref[0] + _rms(out, npost_ref[...] * gate)


def _attend(q, k, lt, ctx, w_uvt, ga, pp, x, mod, mod_row, npost, w_out, *, tq, pack):
    b, seq, d = x.shape
    has_ctx = ctx is not None
    n_keys = seq + (ctx[0].shape[1] if has_ctx else 0)
    nq = seq // tq

    def const(shape):
        return pl.BlockSpec(shape, lambda j, i: (0,) * len(shape))

    def tiled(width):
        return pl.BlockSpec((1, tq, width), lambda j, i: (j // pack, (j % pack) * nq + i, 0))

    if pack == 1:
        nchunks, kc = lt.shape[1], lt.shape[3]
        lt_spec = pl.BlockSpec((1, nchunks, LAT_ROWS, kc), lambda j, i: (j, 0, 0, 0))
    else:
        assert lt.shape[1] == 1
        lt_spec = pl.BlockSpec((1, 1, LAT_ROWS, seq), lambda j, i: (j // pack, 0, 0, j % pack))
    in_specs = [
        pl.BlockSpec((1, MLA_HEADS, tq, QK_PAD), lambda j, i: (j // pack, 0, (j % pack) * nq + i, 0)),
        pl.BlockSpec((1, seq, QK_PAD), lambda j, i: (j // pack, j % pack, 0)),
        lt_spec,
    ]
    args = [q, k, lt]
    if has_ctx:
        kctx, ltctx = ctx
        past = kctx.shape[1]
        in_specs += [
            pl.BlockSpec((1, past, QK_PAD), lambda j, i: (j, 0, 0)),
            pl.BlockSpec((1, LAT_ROWS, past), lambda j, i: (j, 0, 0)),
        ]
        args += [kctx, ltctx]
    in_specs += [
        const(w_uvt.shape),
        tiled(MLA_WIDTH),
        tiled(POOL_WIDTH),
        pl.BlockSpec((1, tq, d), lambda j, i: (j, i, 0)),
        pl.BlockSpec((1, 1, 3 * d), lambda j, i: (mod_row(j), 0, 0)),
        const((1, d)), const(w_out.shape),
    ]
    args += [w_uvt, ga, pp, x, mod, npost, w_out]
    return pl.pallas_call(
        functools.partial(_attn_kernel, has_ctx=has_ctx),
        grid=(b, nq),
        in_specs=in_specs,
        out_specs=pl.BlockSpec((1, tq, d), lambda j, i: (j, i, 0)),
        out_shape=jax.ShapeDtypeStruct((b, seq, d), F32),
        scratch_shapes=[pltpu.VMEM((tq, MLA_WIDTH), BF16),
                        pltpu.VMEM((n_keys, tq), F32), pltpu.VMEM((n_keys, tq), F32)],
        compiler_params=pltpu.CompilerParams(
            dimension_semantics=("arbitrary", "arbitrary"), vmem_limit_bytes=VMEM_LIMIT),
        name="mla_attend_out",
    )(*args)


def _gmlp_stages(load_x, store_y, mod_ref, npre_ref, win_ref, lng_ref, lnb_ref, ws_ref, bs_ref,
                 wout_ref, npost_ref, z_scr):
    d = D_MODEL
    tm = z_scr.shape[0]
    st = {}

    def project_u():
        shift = mod_ref[0, :, 0:d]
        scale = mod_ref[0, :, d:2 * d]
        st["h"] = (_rms(load_x(), npre_ref[...] * (1.0 + scale)) + shift).astype(BF16)
        st["u"] = _dot(st["h"], win_ref[:, 0:d])

    def project_v():
        st["v"] = _dot(st["h"], win_ref[:, d:2 * d])

    def project_gate():
        st["g"] = _dot(st.pop("h"), win_ref[:, 2 * d:3 * d])

    def mix_and_project():
        v = _gelu_tanh(st.pop("v"))
        mu = jnp.mean(v, axis=-1, keepdims=True)
        vc = v - mu
        var = jnp.mean(vc * vc, axis=-1, keepdims=True)
        vn = (vc * lax.rsqrt(var + EPS) * lng_ref[...] + lnb_ref[...]).astype(BF16)
        us = _gelu_tanh(st.pop("u")) * jax.nn.silu(st.pop("g"))
        for n in range(tm // CHUNK):
            rows = slice(n * CHUNK, (n + 1) * CHUNK)
            for g in range(SGU_GROUPS):
                cols = slice(g * SGU_GROUP_DIM, (g + 1) * SGU_GROUP_DIM)
                sv = _dot(ws_ref[g], vn[rows, cols]) + bs_ref[:, cols]
                z_scr[rows, cols] = (us[rows, cols] * sv).astype(BF16)
        out = _dot(z_scr[...], wout_ref[...])
        gate = mod_ref[0, :, 2 * d:3 * d]
        store_y(load_x() + _rms(out, npost_ref[...] * gate))

    return [project_u, project_v, project_gate, mix_and_project]


def _gmlp_kernel(x_ref, mod_ref, npre_ref, win_ref, lng_ref, lnb_ref, ws_ref, bs_ref, wout_ref,
                 npost_ref, o_ref, z_scr):
    def store_y(y):
        o_ref[0] = y

    for stage in _gmlp_stages(lambda: x_ref[0], store_y, mod_ref, npre_ref, win_ref, lng_ref, lnb_ref,
                              ws_ref, bs_ref, wout_ref, npost_ref, z_scr):
        stage()


def _gmlp(x, mod, mod_row, npre, w_in, lng, lnb, w_s, bias, w_out, npost, *, tm):
    b, seq, d = x.shape

    def const(shape):
        return pl.BlockSpec(shape, lambda j, i: (0,) * len(shape))

    return pl.pallas_call(
        _gmlp_kernel,
        grid=(b, seq // tm),
        in_specs=[
            pl.BlockSpec((1, tm, d), lambda j, i: (j, i, 0)),
            pl.BlockSpec((1, 1, 3 * d), lambda j, i: (mod_row(j), 0, 0)),
            const((1, d)), const(w_in.shape), const((1, d)), const((1, d)),
            const(w_s.shape), const(bias.shape), const(w_out.shape), const((1, d)),
        ],
        out_specs=pl.BlockSpec((1, tm, d), lambda j, i: (j, i, 0)),
        out_shape=jax.ShapeDtypeStruct((b, seq, d), F32),
        scratch_shapes=[pltpu.VMEM((tm, d), BF16)],
        compiler_params=pltpu.CompilerParams(
            dimension_semantics=("arbitrary", "arbitrary"), vmem_limit_bytes=VMEM_LIMIT),
        name="gmlp_layer",
    )(x, mod, npre, w_in, lng, lnb, w_s, bias, w_out, npost)


def _swap16(w):
    half = QK_ROPE // 4
    parts = [w[..., k * half:(k + 1) * half] for k in range(4)]
    return jnp.concatenate([parts[1], parts[0], parts[3], parts[2]], axis=-1)


def _rope_tables(seq):
    t = np.arange(seq)
    half = QK_ROPE // 4
    inv = ROPE_THETA ** (-np.arange(half, dtype=np.float64) / half)
    ang_r = (t // GRID_W)[:, None] * inv
    ang_c = (t % GRID_W)[:, None] * inv
    cr, sr, cc, sc = np.cos(ang_r), np.sin(ang_r), np.cos(ang_c), np.sin(ang_c)
    zero = np.zeros((seq, QK_ROPE))
    cos = np.concatenate([cr, cr, cc, cc, zero], axis=-1)
    sin = np.concatenate([-sr, sr, -sc, sc, zero], axis=-1)
    return jnp.asarray(cos, F32), jnp.asarray(sin, F32)


def _no_position_tables(seq):
    one = np.concatenate([np.ones((seq, QK_ROPE)), np.zeros((seq, QK_ROPE))], axis=-1)
    return jnp.asarray(one, F32), jnp.zeros((seq, 2 * QK_ROPE), F32)


def _absorb_kernel(wqn_ref, wukt_ref, o_ref):
    for hd in range(MLA_HEADS):
        o_ref[hd] = _dot(wqn_ref[hd].astype(BF16), wukt_ref[hd].astype(BF16)).astype(BF16)


def _absorb(wq_nope, w_ukt):
    return pl.pallas_call(
        _absorb_kernel,
        out_shape=jax.ShapeDtypeStruct((MLA_HEADS, Q_LORA, KV_LORA), BF16),
        name="absorb_q",
    )(wq_nope, w_ukt)


def kernel(x_prompt, x_sample, cache_ckv, cache_kpe, c, c_ctx, w_ada, b_ada, norm_pre, norm_post,
           w_in_ap, q_norm, w_uq, kv_norm, w_ukv, w_pool, pool_scale, w_out_ap,
           w_in_c, sgu_ln_g, sgu_ln_b, w_s, b_s, w_out_c):
    d = D_MODEL
    dec_b = x_sample.shape[0]
    ctx_row = dec_b

    cond = jnp.zeros((MOD_ROWS, d), F32).at[:dec_b].set(c).at[ctx_row].set(c_ctx)
    mod = _ada(cond, w_ada, b_ada)
    mod0 = mod[0].reshape(MOD_ROWS, 1, 3 * d)
    mod1 = mod[1].reshape(MOD_ROWS, 1, 3 * d)
    row_sample = lambda j: j
    row_prompt = lambda j: ctx_row

    wi = w_in_ap[0]
    q_lat, kv_lat, k_pe, gate_a, pool_in, gate_b = jnp.split(
        wi, (Q_LORA, Q_LORA + KV_LORA, Q_LORA + KV_LORA + QK_ROPE,
             Q_LORA + KV_LORA + QK_ROPE + MLA_WIDTH,
             Q_LORA + KV_LORA + QK_ROPE + MLA_WIDTH + POOL_WIDTH), axis=1)
    w_in = jnp.concatenate(
        [q_lat, kv_lat, gate_a, pool_in, gate_b, k_pe, _swap16(k_pe)], axis=1).astype(BF16)

    wkv = w_ukv[0].reshape(KV_LORA, MLA_HEADS, QK_NOPE + V_DIM)
    w_ukt = wkv[..., :QK_NOPE].transpose(1, 2, 0)
    w_uvt = wkv[..., QK_NOPE:].transpose(1, 2, 0).astype(BF16)

    wq = w_uq[0].reshape(Q_LORA, MLA_HEADS, QK_NOPE + QK_ROPE)
    wq_n, wq_p = wq[..., :QK_NOPE], wq[..., QK_NOPE:]
    w_qa = _absorb(wq_n.transpose(1, 0, 2), w_ukt)
    w_q = jnp.concatenate(
        [w_qa.transpose(1, 0, 2).reshape(Q_LORA, -1),
         jnp.concatenate([wq_p, _swap16(wq_p)], axis=-1).reshape(Q_LORA, -1).astype(BF16)], axis=1)

    w_pool_b = w_pool[0].astype(BF16)
    w_out_b = w_out_ap[0].astype(BF16)
    npre0, npost0 = norm_pre[0][None], norm_post[0][None]
    npre1, npost1 = norm_pre[1][None], norm_post[1][None]
    qn, kvn, pscale = q_norm[0][None], kv_norm[0][None], pool_scale[0][None]

    w_in_c_b = w_in_c[0].astype(BF16)
    w_s_b = w_s[0].astype(BF16)
    bias = jnp.repeat(b_s[0].T, SGU_GROUP_DIM, axis=1)
    w_out_c_b = w_out_c[0].astype(BF16)
    lng, lnb = sgu_ln_g[0][None], sgu_ln_b[0][None]

    n_p, seq_p = x_prompt.shape[0], x_prompt.shape[1]
    packed = x_prompt.reshape(n_p // PROMPT_PACK, PROMPT_PACK * seq_p, d)
    qp, kp, ltp, gap, ppp, ckv_new, kpe_new = _front(
        packed, mod0, row_prompt, npre0, w_in, qn, w_q, kvn, w_pool_b, pscale,
        _no_position_tables(PROMPT_PACK * seq_p), tm=PROMPT_PACK * seq_p, seq_len=seq_p, emit_cache=True)
    ckv_new = ckv_new.reshape(n_p, seq_p, KV_LORA)
    kpe_new = kpe_new.reshape(n_p, seq_p, QK_ROPE)
    xp1 = _attend(qp, kp, ltp, None, w_uvt, gap, ppp, x_prompt, mod0, row_prompt, npost0, w_out_b,
                  tq=seq_p, pack=PROMPT_PACK)
    pair = xp1.reshape(xp1.shape[0] // 2, 2 * seq_p, d)
    y_prompt = _gmlp(pair, mod1, row_prompt, npre1, w_in_c_b, lng, lnb, w_s_b, bias, w_out_c_b, npost1,
                     tm=2 * seq_p).reshape(xp1.shape)

    seq_s = x_sample.shape[1]
    kpe_pad = jnp.pad(cache_kpe[:, 0], ((0, 0), (0, 0), (0, QK_ROPE)))
    ctx = _ctx_keys(cache_ckv[:, 0], kpe_pad)
    qs, ks, lts, gas, pps = _front(
        x_sample, mod0, row_sample, npre0, w_in, qn, w_q, kvn,
        w_pool_b, pscale, _rope_tables(seq_s), tm=1024, seq_len=seq_s, emit_cache=False)
    xs1 = _attend(qs, ks, lts, ctx, w_uvt, gas, pps, x_sample, mod0, row_sample, npost0, w_out_b,
                  tq=512, pack=1)
    y_sample = _gmlp(xs1, mod1, row_sample, npre1, w_in_c_b, lng, lnb, w_s_b, bias, w_out_c_b, npost1,
                     tm=512)

    return (y_prompt, y_sample, ckv_new[:, None], kpe_new[:, None])
```

```python
import functools

import jax
import jax.numpy as jnp
import numpy as np
from jax import lax
from jax.experimental import pallas as pl
from jax.experimental.pallas import tpu as pltpu

D_MODEL = 1024
EPS = 1e-6
MLA_HEADS = 4
Q_LORA = 256
KV_LORA = 128
QK_NOPE = 128
QK_ROPE = 64
V_DIM = 128
ROPE_THETA = 10000.0
GRID_W = 64
POOL_WINDOWS = (2, 4, 8, 16)
POOL_GROUP = 128
POOL_WIDTH = len(POOL_WINDOWS) * POOL_GROUP
MLA_WIDTH = MLA_HEADS * V_DIM
CHUNK = 128
SGU_GROUPS = 4
SGU_GROUP_DIM = D_MODEL // SGU_GROUPS
LOG2_E = 1.4426950408889634
Q_SCALE = (QK_NOPE + QK_ROPE) ** -0.5 * LOG2_E

QK_PAD = KV_LORA + 2 * QK_ROPE
BF16_SUBLANES = 16
LAT_ROWS = KV_LORA + BF16_SUBLANES
GMLP_ROW_GROUPS = 2
FRONT_ROW_GROUPS = 4
PROMPT_PACK = 4
KEY_BLOCK = 1024
POOL_HALO = 8
MOD_ROWS = 16
V7X_VMEM_BYTES = 64 * 1024 * 1024
VMEM_LIMIT = V7X_VMEM_BYTES - 8 * 1024 * 1024

_C_QLAT = 0
_C_KVLAT = _C_QLAT + Q_LORA
_C_GATE_A = _C_KVLAT + KV_LORA
_C_POOL = _C_GATE_A + MLA_WIDTH
_C_GATE_B = _C_POOL + POOL_WIDTH
_C_KPE = _C_GATE_B + POOL_WIDTH
_C_END = _C_KPE + 2 * QK_ROPE

BF16 = jnp.bfloat16
F32 = jnp.float32


def _dot(a, b):
    return jnp.dot(a, b, preferred_element_type=F32)


def _dot_nt(a, b):
    return lax.dot_general(a, b, (((1,), (1,)), ((), ())), preferred_element_type=F32)


def _gelu_tanh(x):
    c = np.sqrt(2.0 / np.pi)
    hx = 0.5 * x
    return hx + hx * jnp.tanh(x * (c + (c * 0.044715) * (x * x)))


def _rms(x, g):
    return x * lax.rsqrt(jnp.mean(x * x, axis=-1, keepdims=True) + EPS) * g


def _ada_kernel(c_ref, w_ref, b_ref, o_ref):
    a = jax.nn.silu(c_ref[...]).astype(BF16)
    o_ref[0] = _dot(a, w_ref[0].astype(BF16)) + b_ref[0]


def _ada(cond, w_ada, b_ada):
    depth, d, n = w_ada.shape
    bn = 512
    return pl.pallas_call(
        _ada_kernel,
        grid=(depth, n // bn),
        in_specs=[
            pl.BlockSpec((MOD_ROWS, d), lambda l, j: (0, 0)),
            pl.BlockSpec((1, d, bn), lambda l, j: (l, 0, j)),
            pl.BlockSpec((1, 1, bn), lambda l, j: (l, 0, j)),
        ],
        out_specs=pl.BlockSpec((1, MOD_ROWS, bn), lambda l, j: (l, 0, j)),
        out_shape=jax.ShapeDtypeStruct((depth, MOD_ROWS, n), F32),
        name="ada_mod",
    )(cond, w_ada, b_ada.reshape(depth, 1, n))


def _front_kernel(*refs, tm, seq, halo, emit_cache):
    it = iter(refs)
    x_ref = next(it)
    xp_ref = next(it) if halo else None
    xn_ref = next(it) if halo else None
    mod_ref = next(it)
    npre_ref = next(it)
    win_ref = next(it)
    qn_ref = next(it)
    wq_ref = next(it)
    kvn_ref = next(it)
    wpool_ref = next(it)
    pscale_ref = next(it)
    cos_ref = next(it)
    sin_ref = next(it)
    q_ref = next(it)
    k_ref = next(it)
    lt_ref = next(it)
    ga_ref = next(it)
    pp_ref = next(it)
    ckv_ref = next(it) if emit_cache else None
    kpe_ref = next(it) if emit_cache else None

    i = pl.program_id(0)
    nt = pl.num_programs(0)
    d = D_MODEL
    shift = mod_ref[0, :, 0:d]
    scale = mod_ref[0, :, d:2 * d]
    gain = npre_ref[...] * (1.0 + scale)

    def modulate(xv):
        return (_rms(xv, gain) + shift).astype(BF16)

    def rotate(blk):
        return blk * cos_ref[...] + pltpu.roll(blk, QK_ROPE, axis=1) * sin_ref[...]

    rg = tm // FRONT_ROW_GROUPS
    h_parts = [modulate(x_ref[0, r:r + rg]) for r in range(0, tm, rg)]

    def project(c0, c1):
        return jnp.concatenate([_dot(hp, win_ref[:, c0:c1]) for hp in h_parts], axis=0)

    q_lat = project(_C_QLAT, _C_KVLAT)
    qn = _rms(q_lat, qn_ref[...]).astype(BF16)
    qa = _dot(qn, wq_ref[:, 0:MLA_HEADS * KV_LORA]) * Q_SCALE
    qp = _dot(qn, wq_ref[:, MLA_HEADS * KV_LORA:]) * Q_SCALE
    for hd in range(MLA_HEADS):
        q_ref[0, hd, :, 0:KV_LORA] = qa[:, hd * KV_LORA:(hd + 1) * KV_LORA].astype(BF16)
        q_ref[0, hd, :, KV_LORA:QK_PAD] = rotate(
            qp[:, hd * 2 * QK_ROPE:(hd + 1) * 2 * QK_ROPE]).astype(BF16)

    ckv = _rms(project(_C_KVLAT, _C_GATE_A), kvn_ref[...])
    kpe2 = project(_C_KPE, _C_END)
    if emit_cache:
        ckv_ref[0] = ckv
        kpe_ref[0] = kpe2[:, 0:QK_ROPE]
    k_ref[0, :, 0:KV_LORA] = ckv.astype(BF16)
    k_ref[0, :, KV_LORA:QK_PAD] = rotate(kpe2).astype(BF16)
    lt_ref[0, 0, 0:KV_LORA] = ckv.T.astype(BF16)
    lt_ref[0, 0, KV_LORA:LAT_ROWS] = jnp.ones((BF16_SUBLANES, tm), BF16)

    ga_ref[0] = jax.nn.silu(project(_C_GATE_A, _C_POOL)).astype(BF16)

    u = project(_C_POOL, _C_GATE_B)
    if halo:
        hh = modulate(jnp.concatenate([xp_ref[0], xn_ref[0]], axis=0))
        uh = _dot(hh, win_ref[:, _C_POOL:_C_GATE_B])
        up = jnp.where(i > 0, uh[0:POOL_HALO], 0.0)
        un = jnp.where(i < nt - 1, uh[POOL_HALO:2 * POOL_HALO], 0.0)
    else:
        up = jnp.zeros((POOL_HALO, POOL_WIDTH), F32)
        un = up
    span = min(tm, seq)
    ext = span + 2 * POOL_HALO
    t = (i * tm) % seq + lax.broadcasted_iota(jnp.int32, (span, 1), 0)
    gate_b = project(_C_GATE_B, _C_KPE)
    for g, w in enumerate(POOL_WINDOWS):
        sl = slice(g * POOL_GROUP, (g + 1) * POOL_GROUP)
        cnt = (jnp.minimum(t + w // 2, seq) - jnp.maximum(t - w // 2, 0)).astype(F32)
        deltas = []
        for r0 in range(0, tm, span):
            us = u[r0:r0 + span, sl]
            p = jnp.concatenate([up[:, sl], us, un[:, sl]], axis=0)
            k = 1
            while k < w:
                p = p + pltpu.roll(p, k, axis=0)
                k *= 2
            lead = w // 2 - 1
            if lead:
                p = pltpu.roll(p, ext - lead, axis=0)
            wsum = p[POOL_HALO:POOL_HALO + span]
            deltas.append((wsum * (1.0 / cnt) - us).astype(BF16))
        dlt = deltas[0] if len(deltas) == 1 else jnp.concatenate(deltas, axis=0)
        og = _dot(dlt, wpool_ref[g]) * pscale_ref[:, sl]
        pp_ref[0, :, sl] = (og * jax.nn.silu(gate_b[:, sl])).astype(BF16)


def _front(x, mod, mod_row, npre, w_in, qn, w_q, kvn, w_pool, pscale, tables, *, tm, seq_len,
           emit_cache):
    b, seq, d = x.shape
    assert seq == seq_len or tm == seq
    nt = seq // tm
    halo = nt > 1
    hb = tm // POOL_HALO
    last = seq // POOL_HALO - 1

    def const(shape):
        return pl.BlockSpec(shape, lambda i, j: (0,) * len(shape))

    in_specs = [pl.BlockSpec((1, tm, d), lambda i, j: (j, i, 0))]
    args = [x]
    if halo:
        in_specs += [
            pl.BlockSpec((1, POOL_HALO, d), lambda i, j: (j, jnp.maximum(i * hb - 1, 0), 0)),
            pl.BlockSpec((1, POOL_HALO, d), lambda i, j: (j, jnp.minimum((i + 1) * hb, last), 0)),
        ]
        args += [x, x]
    in_specs += [
        pl.BlockSpec((1, 1, 3 * d), lambda i, j: (mod_row(j), 0, 0)),
        const((1, d)), const(w_in.shape), const((1, Q_LORA)), const(w_q.shape),
        const((1, KV_LORA)), const(w_pool.shape), const((1, POOL_WIDTH)),
        pl.BlockSpec((tm, 2 * QK_ROPE), lambda i, j: (i, 0)),
        pl.BlockSpec((tm, 2 * QK_ROPE), lambda i, j: (i, 0)),
    ]
    args += [mod, npre, w_in, qn, w_q, kvn, w_pool, pscale, *tables]

    out_specs = [
        pl.BlockSpec((1, MLA_HEADS, tm, QK_PAD), lambda i, j: (j, 0, i, 0)),
        pl.BlockSpec((1, tm, QK_PAD), lambda i, j: (j, i, 0)),
        pl.BlockSpec((1, 1, LAT_ROWS, tm), lambda i, j: (j, i, 0, 0)),
        pl.BlockSpec((1, tm, MLA_WIDTH), lambda i, j: (j, i, 0)),
        pl.BlockSpec((1, tm, POOL_WIDTH), lambda i, j: (j, i, 0)),
    ]
    out_shape = [
        jax.ShapeDtypeStruct((b, MLA_HEADS, seq, QK_PAD), BF16),
        jax.ShapeDtypeStruct((b, seq, QK_PAD), BF16),
        jax.ShapeDtypeStruct((b, nt, LAT_ROWS, tm), BF16),
        jax.ShapeDtypeStruct((b, seq, MLA_WIDTH), BF16),
        jax.ShapeDtypeStruct((b, seq, POOL_WIDTH), BF16),
    ]
    if emit_cache:
        out_specs += [
            pl.BlockSpec((1, tm, KV_LORA), lambda i, j: (j, i, 0)),
            pl.BlockSpec((1, tm, QK_ROPE), lambda i, j: (j, i, 0)),
        ]
        out_shape += [
            jax.ShapeDtypeStruct((b, seq, KV_LORA), F32),
            jax.ShapeDtypeStruct((b, seq, QK_ROPE), F32),
        ]
    return pl.pallas_call(
        functools.partial(_front_kernel, tm=tm, seq=seq_len, halo=halo, emit_cache=emit_cache),
        grid=(nt, b),
        in_specs=in_specs,
        out_specs=out_specs,
        out_shape=out_shape,
        compiler_params=pltpu.CompilerParams(
            dimension_semantics=("arbitrary", "arbitrary"), vmem_limit_bytes=VMEM_LIMIT),
        name="mla_pool_front",
    )(*args)


def _ctx_kernel(ckv_ref, kpe_ref, k_ref, lt_ref):
    ckv = ckv_ref[0]
    k_ref[0, :, 0:KV_LORA] = ckv.astype(BF16)
    k_ref[0, :, KV_LORA:QK_PAD] = kpe_ref[0].astype(BF16)
    lt_ref[0, 0:KV_LORA] = ckv.T.astype(BF16)
    lt_ref[0, KV_LORA:LAT_ROWS] = jnp.ones((BF16_SUBLANES, ckv.shape[0]), BF16)


def _ctx_keys(ckv, kpe_pad):
    b, past, _ = ckv.shape
    return pl.pallas_call(
        _ctx_kernel,
        grid=(b,),
        in_specs=[
            pl.BlockSpec((1, past, KV_LORA), lambda j: (j, 0, 0)),
            pl.BlockSpec((1, past, 2 * QK_ROPE), lambda j: (j, 0, 0)),
        ],
        out_specs=[
            pl.BlockSpec((1, past, QK_PAD), lambda j: (j, 0, 0)),
            pl.BlockSpec((1, LAT_ROWS, past), lambda j: (j, 0, 0)),
        ],
        out_shape=[
            jax.ShapeDtypeStruct((b, past, QK_PAD), BF16),
            jax.ShapeDtypeStruct((b, LAT_ROWS, past), BF16),
        ],
        name="ctx_keys",
    )(ckv, kpe_pad)


def _key_blocks(k_ref, lt_ref, kc_ref, ltc_ref):
    nchunks, kc = lt_ref.shape[1], lt_ref.shape[3]
    kb = min(KEY_BLOCK, kc)
    blocks = []
    for c in range(nchunks):
        for r in range(0, kc, kb):
            blocks.append((kb, k_ref.at[0, c * kc + r:c * kc + r + kb, :],
                           lt_ref.at[0, c, :, r:r + kb]))
    if kc_ref is not None:
        blocks.append((kc_ref.shape[1], kc_ref.at[0], ltc_ref.at[0]))
    return blocks


def _attention_tile(q_ref, blocks, wuvt_ref, ga_ref, pp_ref, wout_ref, mix_scr, s_bufs):
    tq = q_ref.shape[2]
    sub = 8

    zero = jnp.minimum(pl.program_id(0), 0)
    starts = []
    r0 = 0
    for n, _, _ in blocks:
        starts.append(r0)
        r0 += n

    def rows(bi):
        return pl.ds(pl.multiple_of(starts[bi] + zero, sub), blocks[bi][0])

    def score_head(hd):
        m8 = None
        for bi, (n, kblk, _) in enumerate(blocks):
            s = _dot_nt(kblk[...], q_ref[0, hd])
            s_bufs[hd % 2][rows(bi), :] = s
            cm = jnp.max(s.reshape(n // sub, sub, tq), axis=0)
            m8 = cm if m8 is None else jnp.maximum(m8, cm)
        return m8

    m8 = score_head(0)
    for hd in range(MLA_HEADS):
        m = jnp.max(m8, axis=0, keepdims=True)
        if hd + 1 < MLA_HEADS:
            m8 = score_head(hd + 1)
        else:
            out = _dot(pp_ref[0], wout_ref[MLA_WIDTH:, :])
        acc = None
        for bi, (_, _, ltblk) in enumerate(blocks):
            p = jnp.exp2(s_bufs[hd % 2][rows(bi), :] - m).astype(BF16)
            part = _dot(ltblk[...], p)
            acc = part if acc is None else acc + part
        o_lat = (acc[0:KV_LORA] * (1.0 / acc[KV_LORA:KV_LORA + 1])).astype(BF16)
        o = _dot(wuvt_ref[hd], o_lat).T
        sl = slice(hd * V_DIM, (hd + 1) * V_DIM)
        mix_scr[:, sl] = (o * ga_ref[0, :, sl].astype(F32)).astype(BF16)
    return out + _dot(mix_scr[...], wout_ref[0:MLA_WIDTH, :])


def _attn_kernel(*refs, has_ctx):
    it = iter(refs)
    q_ref = next(it)
    k_ref = next(it)
    lt_ref = next(it)
    kc_ref = next(it) if has_ctx else None
    ltc_ref = next(it) if has_ctx else None
    wuvt_ref = next(it)
    ga_ref = next(it)
    pp_ref = next(it)
    x_ref = next(it)
    mod_ref = next(it)
    npost_ref = next(it)
    wout_ref = next(it)
    o_ref = next(it)
    mix_scr = next(it)
    s_bufs = (next(it), next(it))

    blocks = _key_blocks(k_ref, lt_ref, kc_ref, ltc_ref)
    out = _attention_tile(q_ref, blocks, wuvt_ref, ga_ref, pp_ref, wout_ref, mix_scr, s_bufs)
    gate = mod_ref[0, :, 2 * D_MODEL:3 * D_MODEL]
    o_ref[0] = x_ref[0] + _rms(out, npost_ref[...] * gate)


def _attend(q, k, lt, ctx, w_uvt, ga, pp, x, mod, mod_row, npost, w_out, *, tq, pack):
    b, seq, d = x.shape
    has_ctx = ctx is not None
    n_keys = seq + (ctx[0].shape[1] if has_ctx else 0)
    nq = seq // tq

    def const(shape):
        return pl.BlockSpec(shape, lambda j, i: (0,) * len(shape))

    def tiled(width):
        return pl.BlockSpec((1, tq, width), lambda j, i: (j // pack, (j % pack) * nq + i, 0))

    if pack == 1:
        nchunks, kc = lt.shape[1], lt.shape[3]
        lt_spec = pl.BlockSpec((1, nchunks, LAT_ROWS, kc), lambda j, i: (j, 0, 0, 0))
    else:
        assert lt.shape[1] == 1
        lt_spec = pl.BlockSpec((1, 1, LAT_ROWS, seq), lambda j, i: (j // pack, 0, 0, j % pack))
    in_specs = [
        pl.BlockSpec((1, MLA_HEADS, tq, QK_PAD), lambda j, i: (j // pack, 0, (j % pack) * nq + i, 0)),
        pl.BlockSpec((1, seq, QK_PAD), lambda j, i: (j // pack, j % pack, 0)),
        lt_spec,
    ]
    args = [q, k, lt]
    if has_ctx:
        kctx, ltctx = ctx
        past = kctx.shape[1]
        in_specs += [
            pl.BlockSpec((1, past, QK_PAD), lambda j, i: (j, 0, 0)),
            pl.BlockSpec((1, LAT_ROWS, past), lambda j, i: (j, 0, 0)),
        ]
        args += [kctx, ltctx]
    in_specs += [
        const(w_uvt.shape),
        tiled(MLA_WIDTH),
        tiled(POOL_WIDTH),
        pl.BlockSpec((1, tq, d), lambda j, i: (j, i, 0)),
        pl.BlockSpec((1, 1, 3 * d), lambda j, i: (mod_row(j), 0, 0)),
        const((1, d)), const(w_out.shape),
    ]
    args += [w_uvt, ga, pp, x, mod, npost, w_out]
    return pl.pallas_call(
        functools.partial(_attn_kernel, has_ctx=has_ctx),
        grid=(b, nq),
        in_specs=in_specs,
        out_specs=pl.BlockSpec((1, tq, d), lambda j, i: (j, i, 0)),
        out_shape=jax.ShapeDtypeStruct((b, seq, d), F32),
        scratch_shapes=[pltpu.VMEM((tq, MLA_WIDTH), BF16),
                        pltpu.VMEM((n_keys, tq), F32), pltpu.VMEM((n_keys, tq), F32)],
        compiler_params=pltpu.CompilerParams(
            dimension_semantics=("arbitrary", "arbitrary"), vmem_limit_bytes=VMEM_LIMIT),
        name="mla_attend_out",
    )(*args)


def _gmlp_stages(load_x, store_y, mod_ref, npre_ref, win_ref, lng_ref, lnb_ref, ws_ref, bs_ref,
                 wout_ref, npost_ref, z_scr):
    d = D_MODEL
    tm = z_scr.shape[0]
    st = {}

    def project(c0, c1):
        return jnp.concatenate([_dot(hp, win_ref[:, c0:c1]) for hp in st["h"]], axis=0)

    def project_u():
        shift = mod_ref[0, :, 0:d]
        gain = npre_ref[...] * (1.0 + mod_ref[0, :, d:2 * d])
        x = load_x()
        rg = tm // GMLP_ROW_GROUPS
        st["h"] = [(_rms(x[r:r + rg], gain) + shift).astype(BF16) for r in range(0, tm, rg)]
        st["u"] = project(0, d)

    def project_v():
        st["v"] = project(d, 2 * d)

    def project_gate():
        st["g"] = project(2 * d, 3 * d)

    def mix_and_project():
        v = _gelu_tanh(st.pop("v"))
        mu = jnp.mean(v, axis=-1, keepdims=True)
        vc = v - mu
        var = jnp.mean(vc * vc, axis=-1, keepdims=True)
        vn = (vc * lax.rsqrt(var + EPS) * lng_ref[...] + lnb_ref[...]).astype(BF16)
        us = _gelu_tanh(st.pop("u")) * jax.nn.silu(st.pop("g"))
        for n in range(tm // CHUNK):
            rows = slice(n * CHUNK, (n + 1) * CHUNK)
            for g in range(SGU_GROUPS):
                cols = slice(g * SGU_GROUP_DIM, (g + 1) * SGU_GROUP_DIM)
                sv = _dot(ws_ref[g], vn[rows, cols]) + bs_ref[:, cols]
                z_scr[rows, cols] = (us[rows, cols] * sv).astype(BF16)
        out = _dot(z_scr[...], wout_ref[...])
        gate = mod_ref[0, :, 2 * d:3 * d]
        store_y(load_x() + _rms(out, npost_ref[...] * gate))

    return [project_u, project_v, project_gate, mix_and_project]


def _gmlp_kernel(x_ref, mod_ref, npre_ref, win_ref, lng_ref, lnb_ref, ws_ref, bs_ref, wout_ref,
                 npost_ref, o_ref, z_scr):
    def store_y(y):
        o_ref[0] = y

    for stage in _gmlp_stages(lambda: x_ref[0], store_y, mod_ref, npre_ref, win_ref, lng_ref, lnb_ref,
                              ws_ref, bs_ref, wout_ref, npost_ref, z_scr):
        stage()


def _gmlp(x, mod, mod_row, npre, w_in, lng, lnb, w_s, bias, w_out, npost, *, tm):
    b, seq, d = x.shape

    def const(shape):
        return pl.BlockSpec(shape, lambda j, i: (0,) * len(shape))

    return pl.pallas_call(
        _gmlp_kernel,
        grid=(b, seq // tm),
        in_specs=[
            pl.BlockSpec((1, tm, d), lambda j, i: (j, i, 0)),
            pl.BlockSpec((1, 1, 3 * d), lambda j, i: (mod_row(j), 0, 0)),
            const((1, d)), const(w_in.shape), const((1, d)), const((1, d)),
            const(w_s.shape), const(bias.shape), const(w_out.shape), const((1, d)),
        ],
        out_specs=pl.BlockSpec((1, tm, d), lambda j, i: (j, i, 0)),
        out_shape=jax.ShapeDtypeStruct((b, seq, d), F32),
        scratch_shapes=[pltpu.VMEM((tm, d), BF16)],
        compiler_params=pltpu.CompilerParams(
            dimension_semantics=("arbitrary", "arbitrary"), vmem_limit_bytes=VMEM_LIMIT),
        name="gmlp_layer",
    )(x, mod, npre, w_in, lng, lnb, w_s, bias, w_out, npost)


def _swap16(w):
    half = QK_ROPE // 4
    parts = [w[..., k * half:(k + 1) * half] for k in range(4)]
    return jnp.concatenate([parts[1], parts[0], parts[3], parts[2]], axis=-1)


def _rope_tables(seq):
    t = np.arange(seq)
    half = QK_ROPE // 4
    inv = ROPE_THETA ** (-np.arange(half, dtype=np.float64) / half)
    ang_r = (t // GRID_W)[:, None] * inv
    ang_c = (t % GRID_W)[:, None] * inv
    cr, sr, cc, sc = np.cos(ang_r), np.sin(ang_r), np.cos(ang_c), np.sin(ang_c)
    zero = np.zeros((seq, QK_ROPE))
    cos = np.concatenate([cr, cr, cc, cc, zero], axis=-1)
    sin = np.concatenate([-sr, sr, -sc, sc, zero], axis=-1)
    return jnp.asarray(cos, F32), jnp.asarray(sin, F32)


def _no_position_tables(seq):
    one = np.concatenate([np.ones((seq, QK_ROPE)), np.zeros((seq, QK_ROPE))], axis=-1)
    return jnp.asarray(one, F32), jnp.zeros((seq, 2 * QK_ROPE), F32)


def _absorb_kernel(wqn_ref, wukt_ref, o_ref):
    for hd in range(MLA_HEADS):
        o_ref[hd] = _dot(wqn_ref[hd].astype(BF16), wukt_ref[hd].astype(BF16)).astype(BF16)


def _absorb(wq_nope, w_ukt):
    return pl.pallas_call(
        _absorb_kernel,
        out_shape=jax.ShapeDtypeStruct((MLA_HEADS, Q_LORA, KV_LORA), BF16),
        name="absorb_q",
    )(wq_nope, w_ukt)


def kernel(x_prompt, x_sample, cache_ckv, cache_kpe, c, c_ctx, w_ada, b_ada, norm_pre, norm_post,
           w_in_ap, q_norm, w_uq, kv_norm, w_ukv, w_pool, pool_scale, w_out_ap,
           w_in_c, sgu_ln_g, sgu_ln_b, w_s, b_s, w_out_c):
    d = D_MODEL
    dec_b = x_sample.shape[0]
    ctx_row = dec_b

    cond = jnp.zeros((MOD_ROWS, d), F32).at[:dec_b].set(c).at[ctx_row].set(c_ctx)
    mod = _ada(cond, w_ada, b_ada)
    mod0 = mod[0].reshape(MOD_ROWS, 1, 3 * d)
    mod1 = mod[1].reshape(MOD_ROWS, 1, 3 * d)
    row_sample = lambda j: j
    row_prompt = lambda j: ctx_row

    wi = w_in_ap[0]
    q_lat, kv_lat, k_pe, gate_a, pool_in, gate_b = jnp.split(
        wi, (Q_LORA, Q_LORA + KV_LORA, Q_LORA + KV_LORA + QK_ROPE,
             Q_LORA + KV_LORA + QK_ROPE + MLA_WIDTH,
             Q_LORA + KV_LORA + QK_ROPE + MLA_WIDTH + POOL_WIDTH), axis=1)
    w_in = jnp.concatenate(
        [q_lat, kv_lat, gate_a, pool_in, gate_b, k_pe, _swap16(k_pe)], axis=1).astype(BF16)

    wkv = w_ukv[0].reshape(KV_LORA, MLA_HEADS, QK_NOPE + V_DIM)
    w_ukt = wkv[..., :QK_NOPE].transpose(1, 2, 0)
    w_uvt = wkv[..., QK_NOPE:].transpose(1, 2, 0).astype(BF16)

    wq = w_uq[0].reshape(Q_LORA, MLA_HEADS, QK_NOPE + QK_ROPE)
    wq_n, wq_p = wq[..., :QK_NOPE], wq[..., QK_NOPE:]
    w_qa = _absorb(wq_n.transpose(1, 0, 2), w_ukt)
    w_q = jnp.concatenate(
        [w_qa.transpose(1, 0, 2).reshape(Q_LORA, -1),
         jnp.concatenate([wq_p, _swap16(wq_p)], axis=-1).reshape(Q_LORA, -1).astype(BF16)], axis=1)

    w_pool_b = w_pool[0].astype(BF16)
    w_out_b = w_out_ap[0].astype(BF16)
    npre0, npost0 = norm_pre[0][None], norm_post[0][None]
    npre1, npost1 = norm_pre[1][None], norm_post[1][None]
    qn, kvn, pscale = q_norm[0][None], kv_norm[0][None], pool_scale[0][None]

    w_in_c_b = w_in_c[0].astype(BF16)
    w_s_b = w_s[0].astype(BF16)
    bias = jnp.repeat(b_s[0].T, SGU_GROUP_DIM, axis=1)
    w_out_c_b = w_out_c[0].astype(BF16)
    lng, lnb = sgu_ln_g[0][None], sgu_ln_b[0][None]

    n_p, seq_p = x_prompt.shape[0], x_prompt.shape[1]
    packed = x_prompt.reshape(n_p // PROMPT_PACK, PROMPT_PACK * seq_p, d)
    qp, kp, ltp, gap, ppp, ckv_new, kpe_new = _front(
        packed, mod0, row_prompt, npre0, w_in, qn, w_q, kvn, w_pool_b, pscale,
        _no_position_tables(PROMPT_PACK * seq_p), tm=PROMPT_PACK * seq_p, seq_len=seq_p, emit_cache=True)
    ckv_new = ckv_new.reshape(n_p, seq_p, KV_LORA)
    kpe_new = kpe_new.reshape(n_p, seq_p, QK_ROPE)
    xp1 = _attend(qp, kp, ltp, None, w_uvt, gap, ppp, x_prompt, mod0, row_prompt, npost0, w_out_b,
                  tq=seq_p, pack=PROMPT_PACK)
    pair = xp1.reshape(xp1.shape[0] // 2, 2 * seq_p, d)
    y_prompt = _gmlp(pair, mod1, row_prompt, npre1, w_in_c_b, lng, lnb, w_s_b, bias, w_out_c_b, npost1,
                     tm=2 * seq_p).reshape(xp1.shape)

    seq_s = x_sample.shape[1]
    kpe_pad = jnp.pad(cache_kpe[:, 0], ((0, 0), (0, 0), (0, QK_ROPE)))
    ctx = _ctx_keys(cache_ckv[:, 0], kpe_pad)
    qs, ks, lts, gas, pps = _front(
        x_sample, mod0, row_sample, npre0, w_in, qn, w_q, kvn,
        w_pool_b, pscale, _rope_tables(seq_s), tm=1024, seq_len=seq_s, emit_cache=False)
    xs1 = _attend(qs, ks, lts, ctx, w_uvt, gas, pps, x_sample, mod0, row_sample, npost0, w_out_b,
                  tq=512, pack=1)
    y_sample = _gmlp(xs1, mod1, row_sample, npre1, w_in_c_b, lng, lnb, w_s_b, bias, w_out_c_b, npost1,
                     tm=512)

    return (y_prompt, y_sample, ckv_new[:, None], kpe_new[:, None])
```
